```python
import math, functools
import jax, jax.numpy as jnp
from jax import lax
import numpy as np

D_MODEL = 2048
BATCH = 4
SEQ = 8192
DEPTH = 2

GRID_W = 64
CTX_LEN = 256
EPS = 1e-6
HG_HEADS = 8
HG_DK = 128
HG_DV = 128
HG_WIDTH = HG_HEADS * HG_DV
HG_CHUNK = 64
SSM_HEADS = 16
SSM_HEAD_DIM = 64
SSM_WIDTH = SSM_HEADS * SSM_HEAD_DIM
SSM_GROUPS = 4
SSM_STATE = 128
SSM_CONV = 5
SSM_CONV_CH = SSM_WIDTH + 2 * SSM_GROUPS * SSM_STATE
SSM_CHUNK = 64
ATT_HEADS = 8
ATT_KV_HEADS = 2
ATT_HEAD_DIM = 128
ATT_WIDTH = ATT_HEADS * ATT_HEAD_DIM
Q_BLOCK = 128
ROPE_THETA = 10000.0
N_BRANCHES = 3
BRANCH_W = 1024
D_FF = 5632
N_EXPERTS = 8
TOP_K = 2
N_DENSE = (DEPTH + 1) // 2
N_MOE = DEPTH // 2
SPLIT_SIZES = (
    HG_HEADS * HG_DK,
    HG_HEADS * HG_DK,
    HG_HEADS * HG_DK,
    HG_WIDTH,
    HG_WIDTH,
    SSM_WIDTH,
    SSM_CONV_CH,
    2 * SSM_HEADS,
    ATT_WIDTH,
    ATT_KV_HEADS * ATT_HEAD_DIM,
    ATT_KV_HEADS * ATT_HEAD_DIM,
    N_BRANCHES * D_MODEL,
)
IN_COLS = sum(SPLIT_SIZES)

kernel_name = "hybrid_flow_hgrn2_ssd_gqa_moe"


def rms_norm(x, w):
    xf = x.astype(jnp.float32)
    y = xf * lax.rsqrt(jnp.mean(xf * xf, axis=-1, keepdims=True) + EPS)
    return (y * w.astype(jnp.float32)).astype(x.dtype)


def grouped_rms_norm(x, w, groups):
    xs = x.reshape(x.shape[:-1] + (groups, x.shape[-1] // groups))
    return rms_norm(xs, w.reshape(groups, -1)).reshape(x.shape)


def modulate(x, w, shift, scale):
    return rms_norm(x, w) * (1 + scale) + shift


def split_cols(p):
    idx = np.cumsum(SPLIT_SIZES)[:-1].tolist()
    return jnp.split(p, idx, axis=-1)


def depthwise_conv_centred(x, w, b):
    k = w.shape[0]
    y = lax.conv_general_dilated(
        x, w[:, None, :].astype(x.dtype), window_strides=(1,), padding=((k // 2, k // 2),),
        dimension_numbers=("NWC", "WIO", "NWC"), feature_group_count=x.shape[-1])
    return y + b.astype(x.dtype)


def hgrn2_chunk_scan(q, k, v, log_f, s0):
    b_, t_, h_, _ = q.shape
    nc = t_ // HG_CHUNK

    def chunks(a):
        return a.reshape(b_, nc, HG_CHUNK, h_, a.shape[-1]).transpose(1, 0, 3, 2, 4)

    incl = jnp.tril(jnp.ones((HG_CHUNK, HG_CHUNK), dtype=bool))[:, :, None]

    def step(state, inp):
        qc, kc, vc, gc = inp
        cum = jnp.cumsum(gc, axis=2)
        rel = jnp.where(incl, cum[:, :, :, None, :] - cum[:, :, None, :, :], -jnp.inf)
        scores = jnp.einsum("bhtk,bhtsk->bhts", qc, kc[:, :, None] * jnp.exp(rel))
        o = (jnp.einsum("bhts,bhsv->bhtv", scores, vc)
             + jnp.einsum("bhtk,bhkv->bhtv", qc * jnp.exp(cum), state))
        end = cum[:, :, -1:, :]
        state = (jnp.exp(end[:, :, 0, :, None]) * state
                 + jnp.einsum("bhsk,bhsv->bhkv", kc * jnp.exp(end - cum), vc))
        return state, o

    s_final, o = lax.scan(step, s0, (chunks(q), chunks(k), chunks(v), chunks(log_f)))
    o = o.transpose(1, 0, 3, 2, 4).reshape(b_, t_, h_, v.shape[-1])
    return o, s_final


def hgrn2_direction(q, f_raw, v, lb, s0, reverse):
    if reverse:
        q, f_raw, v = jnp.flip(q, 1), jnp.flip(f_raw, 1), jnp.flip(v, 1)
    log_f = jnp.logaddexp(jnp.log(lb), jnp.log1p(-lb) + jax.nn.log_sigmoid(f_raw))
    k = (1 - lb) * jax.nn.sigmoid(-f_raw)
    o, s = hgrn2_chunk_scan(q, k, v, log_f, s0)
    if reverse:
        o = jnp.flip(o, 1)
    return o, s


def hgrn2_mixer(parts_lat, parts_ctx, lb, norm_w, need_ctx):
    def prep(parts):
        q, f_fwd, f_bwd, i, g = parts
        b_, t_ = q.shape[:2]
        hd = lambda a, d: a.reshape(b_, t_, HG_HEADS, d).astype(jnp.float32)
        return hd(q, HG_DK) * HG_DK ** -0.5, hd(f_fwd, HG_DK), hd(f_bwd, HG_DK), hd(i, HG_DV), g

    q_c, ff_c, fb_c, i_c, g_c = prep(parts_ctx)
    q_l, ff_l, fb_l, i_l, g_l = prep(parts_lat)
    s0 = jnp.zeros((q_c.shape[0], HG_HEADS, HG_DK, HG_DV), jnp.float32)
    o_ctx, o_lat = 0.0, 0.0
    for d, (f_c, f_l) in enumerate(((ff_c, ff_l), (fb_c, fb_l))):
        lbd = lb[d].reshape(HG_HEADS, HG_DK).astype(jnp.float32)
        oc, s_ctx = hgrn2_direction(q_c, f_c, i_c, lbd, s0, d == 1)
        ol, _ = hgrn2_direction(q_l, f_l, i_l, lbd, s_ctx, d == 1)
        o_ctx, o_lat = o_ctx + oc, o_lat + ol

    def readout(o, g):
        o = rms_norm(o, norm_w.reshape(HG_HEADS, HG_DV)).reshape(g.shape)
        return (o * jax.nn.silu(g.astype(jnp.float32))).astype(g.dtype)

    return readout(o_lat, g_l), (readout(o_ctx, g_c) if need_ctx else None)


def segsum(a):
    n = a.shape[-1]
    rep = jnp.broadcast_to(a[..., :, None], a.shape + (n,))
    strict = jnp.tril(jnp.ones((n, n), dtype=bool), -1)
    cs = jnp.cumsum(jnp.where(strict, rep, 0), axis=-2)
    return jnp.where(jnp.tril(jnp.ones((n, n), dtype=bool)), cs, -jnp.inf)


def ssd_chunk_scan(x, dt, a_neg, bm, cm, d_skip, h0):
    b_, t_, h_, p_ = x.shape
    g_, n_ = bm.shape[2], bm.shape[3]
    r_ = h_ // g_
    nc = t_ // SSM_CHUNK
    xc = (x * dt[..., None]).reshape(b_, nc, SSM_CHUNK, g_, r_, p_)
    ac = (dt * a_neg).reshape(b_, nc, SSM_CHUNK, g_, r_).transpose(0, 3, 4, 1, 2)
    bc = bm.reshape(b_, nc, SSM_CHUNK, g_, n_)
    cc = cm.reshape(b_, nc, SSM_CHUNK, g_, n_)
    a_cum = jnp.cumsum(ac, axis=-1)
    cb = jnp.einsum("bclgn,bcsgn->bgcls", cc, bc)
    y_diag = jnp.einsum("bgcls,bgrcls,bcsgrp->bclgrp", cb, jnp.exp(segsum(ac)), xc)
    decay_states = jnp.exp(a_cum[..., -1:] - a_cum)
    states = jnp.einsum("bclgn,bgrcl,bclgrp->bcgrpn", bc, decay_states, xc)
    states = jnp.concatenate([h0.reshape(b_, 1, g_, r_, p_, n_), states], axis=1)
    a_end = jnp.pad(a_cum[..., -1], ((0, 0), (0, 0), (0, 0), (1, 0)))
    states = jnp.einsum("bgrzc,bcgrpn->bzgrpn", jnp.exp(segsum(a_end)), states)
    prev_states, final = states[:, :-1], states[:, -1]
    y_off = jnp.einsum("bclgn,bcgrpn,bgrcl->bclgrp", cc, prev_states, jnp.exp(a_cum))
    y = (y_diag + y_off).reshape(b_, t_, h_, p_) + x * d_skip[:, None]
    return y, final.reshape(b_, h_, p_, n_)


def ssd_direction(x, dt, a_neg, bm, cm, d_skip, h0, reverse):
    if reverse:
        x, dt, bm, cm = jnp.flip(x, 1), jnp.flip(dt, 1), jnp.flip(bm, 1), jnp.flip(cm, 1)
    y, h = ssd_chunk_scan(x, dt, a_neg, bm, cm, d_skip, h0)
    if reverse:
        y = jnp.flip(y, 1)
    return y, h


def ssd_mixer(parts_lat, parts_ctx, conv_w, conv_b, dt_bias, a_log, d_skip, norm_w, need_ctx):
    def prep(parts):
        z, xbc, dt_raw = parts
        xbc = jax.nn.silu(depthwise_conv_centred(xbc, conv_w, conv_b)).astype(jnp.float32)
        xs, bm, cm = jnp.split(xbc, [SSM_WIDTH, SSM_WIDTH + SSM_GROUPS * SSM_STATE], axis=-1)
        b_, t_ = xs.shape[:2]
        return (z, xs.reshape(b_, t_, SSM_HEADS, SSM_HEAD_DIM),
                bm.reshape(b_, t_, SSM_GROUPS, SSM_STATE), cm.reshape(b_, t_, SSM_GROUPS, SSM_STATE),
                dt_raw.astype(jnp.float32))

    z_c, x_c, b_c, c_c, dtr_c = prep(parts_ctx)
    z_l, x_l, b_l, c_l, dtr_l = prep(parts_lat)
    h0 = jnp.zeros((x_c.shape[0], SSM_HEADS, SSM_HEAD_DIM, SSM_STATE), jnp.float32)
    y_ctx, y_lat = 0.0, 0.0
    for d in range(2):
        bias = dt_bias[d].astype(jnp.float32)
        a_neg = -jnp.exp(a_log[d].astype(jnp.float32))
        dsk = d_skip[d].astype(jnp.float32)
        dt_c = jax.nn.softplus(dtr_c[..., d * SSM_HEADS:(d + 1) * SSM_HEADS] + bias)
        dt_l = jax.nn.softplus(dtr_l[..., d * SSM_HEADS:(d + 1) * SSM_HEADS] + bias)
        yc, h_ctx = ssd_direction(x_c, dt_c, a_neg, b_c, c_c, dsk, h0, d == 1)
        yl, _ = ssd_direction(x_l, dt_l, a_neg, b_l, c_l, dsk, h_ctx, d == 1)
        y_ctx, y_lat = y_ctx + yc, y_lat + yl

    def readout(y, z):
        y = y.reshape(z.shape) * jax.nn.silu(z.astype(jnp.float32))
        return grouped_rms_norm(y, norm_w, SSM_GROUPS).astype(z.dtype)

    return readout(y_lat, z_l), (readout(y_ctx, z_c) if need_ctx else None)


def axial_rope_angles(n_tokens):
    rows = n_tokens // GRID_W
    row = jnp.repeat(jnp.arange(rows, dtype=jnp.float32), GRID_W)
    col = jnp.tile(jnp.arange(GRID_W, dtype=jnp.float32), rows)
    axis_dim = ATT_HEAD_DIM // 2
    inv = ROPE_THETA ** (-jnp.arange(0, axis_dim, 2, dtype=jnp.float32) / axis_dim)
    return row[:, None] * inv, col[:, None] * inv


def apply_axial_rope(x, ang_row, ang_col):
    def rot(xa, ang):
        cos, sin = jnp.cos(ang)[:, None, :], jnp.sin(ang)[:, None, :]
        x1, x2 = jnp.split(xa, 2, axis=-1)
        return jnp.concatenate([x1 * cos - x2 * sin, x1 * sin + x2 * cos], axis=-1).astype(xa.dtype)
    half = x.shape[-1] // 2
    return jnp.concatenate([rot(x[..., :half], ang_row), rot(x[..., half:], ang_col)], axis=-1)


def gqa_attend(q, k, v):
    b_, tq, hq, dh = q.shape
    hkv = k.shape[2]
    grp = hq // hkv
    nb = tq // Q_BLOCK
    qb = q.reshape(b_, nb, Q_BLOCK, hkv, grp, dh).transpose(1, 0, 2, 3, 4, 5)
    scale = dh ** -0.5

    def block(qi):
        s = jnp.einsum("bqkgd,bskd->bkgqs", qi, k).astype(jnp.float32) * scale
        p = jax.nn.softmax(s, axis=-1).astype(v.dtype)
        return jnp.einsum("bkgqs,bskd->bqkgd", p, v)

    o = lax.map(block, qb)
    return o.transpose(1, 0, 2, 3, 4, 5).reshape(b_, tq, hq * dh)


def attention_mixer(parts_lat, parts_ctx, q_norm_w, k_norm_w, need_ctx):
    def heads(parts):
        q, k, v = parts
        b_, t_ = q.shape[:2]
        q = rms_norm(q.reshape(b_, t_, ATT_HEADS, ATT_HEAD_DIM), q_norm_w)
        k = rms_norm(k.reshape(b_, t_, ATT_KV_HEADS, ATT_HEAD_DIM), k_norm_w)
        return q, k, v.reshape(b_, t_, ATT_KV_HEADS, ATT_HEAD_DIM)

    q_c, k_c, v_c = heads(parts_ctx)
    q_l, k_l, v_l = heads(parts_lat)
    ang_row, ang_col = axial_rope_angles(q_l.shape[1])
    q_l = apply_axial_rope(q_l, ang_row, ang_col)
    k_l = apply_axial_rope(k_l, ang_row, ang_col)
    y_lat = gqa_attend(q_l, jnp.concatenate([k_c, k_l], axis=1), jnp.concatenate([v_c, v_l], axis=1))
    y_ctx = gqa_attend(q_c, k_c, v_c) if need_ctx else None
    return y_lat, y_ctx


def merge_branches(ys, gate_logits, w_branch, w_out):
    g = jax.nn.sigmoid(gate_logits)
    merged = sum(g[..., n * D_MODEL:(n + 1) * D_MODEL] * (ys[n] @ w_branch[n]) for n in range(N_BRANCHES))
    return merged @ w_out


def hybrid_mixer(h_lat, h_ctx, need_ctx, w_in, lb, hg_norm_w, conv_w, conv_b, dt_bias, a_log, d_skip,
                 ssm_norm_w, q_norm_w, k_norm_w, w_branch, w_out):
    p_lat = split_cols(h_lat @ w_in)
    p_ctx = split_cols(h_ctx @ w_in)
    ya_l, ya_c = hgrn2_mixer(p_lat[0:5], p_ctx[0:5], lb, hg_norm_w, need_ctx)
    yb_l, yb_c = ssd_mixer(p_lat[5:8], p_ctx[5:8], conv_w, conv_b, dt_bias, a_log, d_skip, ssm_norm_w, need_ctx)
    yc_l, yc_c = attention_mixer(p_lat[8:11], p_ctx[8:11], q_norm_w, k_norm_w, need_ctx)
    out_lat = merge_branches((ya_l, yb_l, yc_l), p_lat[11], w_branch, w_out)
    out_ctx = merge_branches((ya_c, yb_c, yc_c), p_ctx[11], w_branch, w_out) if need_ctx else None
    return out_lat, out_ctx


def swiglu(h, w_gate, w_up, w_down):
    return (jax.nn.silu(h @ w_gate) * (h @ w_up)) @ w_down


def moe_swiglu(h, router_w, w_gate, w_up, w_down):
    logits = (h @ router_w).astype(jnp.float32)
    top_val, top_idx = lax.top_k(logits, TOP_K)
    top_w = jax.nn.softmax(top_val, axis=-1)
    gates = jnp.sum(jax.nn.one_hot(top_idx, N_EXPERTS, dtype=jnp.float32) * top_w[..., None], axis=-2)
    gates = gates.astype(h.dtype)
    out = jnp.zeros_like(h)
    for e in range(N_EXPERTS):
        out = out + gates[..., e:e + 1] * swiglu(h, w_gate[e], w_up[e], w_down[e])
    return out


def setup_inputs(seed: int = 0) -> dict:
    key = jax.random.key(seed)
    ks = jax.random.split(key, 29)
    nrm = jax.random.normal
    d = D_MODEL
    dt0 = jnp.exp(jax.random.uniform(ks[13], (DEPTH, 2, SSM_HEADS), minval=math.log(1e-3), maxval=math.log(1e-1)))
    return {
        "x": nrm(ks[0], (BATCH, SEQ, d), jnp.float32),
        "c": nrm(ks[1], (BATCH, d), jnp.float32),
        "ctx": nrm(ks[2], (BATCH, CTX_LEN, d), jnp.float32),
        "c_ctx": nrm(ks[3], (d,), jnp.float32),
        "ada_w": nrm(ks[4], (DEPTH, d, 6 * d), jnp.float32) * d ** -0.5,
        "ada_b": nrm(ks[5], (DEPTH, 6 * d), jnp.float32) * 0.02,
        "norm_mix_w": 1.0 + 0.05 * nrm(ks[6], (DEPTH, d), jnp.float32),
        "norm_ffn_w": 1.0 + 0.05 * nrm(ks[7], (DEPTH, d), jnp.float32),
        "w_in": nrm(ks[8], (DEPTH, d, IN_COLS), jnp.float32) * d ** -0.5,
        "hg_lb_logits": nrm(ks[9], (DEPTH, 2, HG_HEADS * HG_DK), jnp.float32),
        "hg_norm_w": 1.0 + 0.05 * nrm(ks[10], (DEPTH, HG_WIDTH), jnp.float32),
        "ssm_conv_w": nrm(ks[11], (DEPTH, SSM_CONV, SSM_CONV_CH), jnp.float32) * SSM_CONV ** -0.5,
        "ssm_conv_b": nrm(ks[12], (DEPTH, SSM_CONV_CH), jnp.float32) * 0.02,
        "ssm_dt_bias": dt0 + jnp.log(-jnp.expm1(-dt0)),
        "ssm_a_log": jnp.log(jax.random.uniform(ks[14], (DEPTH, 2, SSM_HEADS), minval=1.0, maxval=16.0)),
        "ssm_d": 1.0 + 0.1 * nrm(ks[15], (DEPTH, 2, SSM_HEADS), jnp.float32),
        "ssm_norm_w": 1.0 + 0.05 * nrm(ks[16], (DEPTH, SSM_WIDTH), jnp.float32),
        "attn_q_norm_w": 1.0 + 0.05 * nrm(ks[17], (DEPTH, ATT_HEAD_DIM), jnp.float32),
        "attn_k_norm_w": 1.0 + 0.05 * nrm(ks[18], (DEPTH, ATT_HEAD_DIM), jnp.float32),
        "w_branch": nrm(ks[19], (DEPTH, N_BRANCHES, BRANCH_W, d), jnp.float32) * BRANCH_W ** -0.5,
        "w_out": nrm(ks[20], (DEPTH, d, d), jnp.float32) * d ** -0.5,
        "ffn_w_gate": nrm(ks[21], (N_DENSE, d, D_FF), jnp.float32) * d ** -0.5,
        "ffn_w_up": nrm(ks[22], (N_DENSE, d, D_FF), jnp.float32) * d ** -0.5,
        "ffn_w_down": nrm(ks[23], (N_DENSE, D_FF, d), jnp.float32) * D_FF ** -0.5,
        "moe_router": nrm(ks[24], (N_MOE, d, N_EXPERTS), jnp.float32) * d ** -0.5,
        "moe_w_gate": nrm(ks[25], (N_MOE, N_EXPERTS, d, D_FF), jnp.float32) * d ** -0.5,
        "moe_w_up": nrm(ks[26], (N_MOE, N_EXPERTS, d, D_FF), jnp.float32) * d ** -0.5,
        "moe_w_down": nrm(ks[27], (N_MOE, N_EXPERTS, D_FF, d), jnp.float32) * D_FF ** -0.5,
        "final_norm_w": 1.0 + 0.05 * nrm(ks[28], (d,), jnp.float32),
    }


def reference(x, c, ctx, c_ctx, ada_w, ada_b, norm_mix_w, norm_ffn_w, w_in, hg_lb_logits, hg_norm_w,
              ssm_conv_w, ssm_conv_b, ssm_dt_bias, ssm_a_log, ssm_d, ssm_norm_w, attn_q_norm_w, attn_k_norm_w,
              w_branch, w_out, ffn_w_gate, ffn_w_up, ffn_w_down, moe_router, moe_w_gate, moe_w_up, moe_w_down,
              final_norm_w):
    lb_all = jnp.cumsum(jax.nn.softmax(hg_lb_logits.astype(jnp.float32), axis=0), axis=0)
    lb_all = lb_all - lb_all[0]
    xc = ctx
    for l in range(DEPTH):
        need_ctx = l < DEPTH - 1
        mod_lat = [m[:, None, :] for m in jnp.split(jax.nn.silu(c) @ ada_w[l] + ada_b[l], 6, axis=-1)]
        mod_ctx = jnp.split(jax.nn.silu(c_ctx) @ ada_w[l] + ada_b[l], 6, axis=-1)
        h_lat = modulate(x, norm_mix_w[l], mod_lat[0], mod_lat[1])
        h_ctx = modulate(xc, norm_mix_w[l], mod_ctx[0], mod_ctx[1])
        m_lat, m_ctx = hybrid_mixer(h_lat, h_ctx, need_ctx, w_in[l], lb_all[l], hg_norm_w[l],
                                    ssm_conv_w[l], ssm_conv_b[l], ssm_dt_bias[l], ssm_a_log[l], ssm_d[l],
                                    ssm_norm_w[l], attn_q_norm_w[l], attn_k_norm_w[l], w_branch[l], w_out[l])
        x = x + mod_lat[2] * m_lat
        if l % 2 == 0:
            ffn = functools.partial(swiglu, w_gate=ffn_w_gate[l // 2], w_up=ffn_w_up[l // 2],
                                    w_down=ffn_w_down[l // 2])
        else:
            ffn = functools.partial(moe_swiglu, router_w=moe_router[l // 2], w_gate=moe_w_gate[l // 2],
                                    w_up=moe_w_up[l // 2], w_down=moe_w_down[l // 2])
        x = x + mod_lat[5] * ffn(modulate(x, norm_ffn_w[l], mod_lat[3], mod_lat[4]))
        if need_ctx:
            xc = xc + mod_ctx[2] * m_ctx
            xc = xc + mod_ctx[5] * ffn(modulate(xc, norm_ffn_w[l], mod_ctx[3], mod_ctx[4]))
    return rms_norm(x, final_norm_w)
```

```python
import functools
import math

import jax
import jax.numpy as jnp
from jax import lax
from jax.experimental import pallas as pl
from jax.experimental.pallas import tpu as pltpu

F32 = jnp.float32
BF16 = jnp.bfloat16

EPS = 1e-6
GRID_W = 64
HG_HEADS = 8
HG_DK = 128
HG_WIDTH = HG_HEADS * HG_DK
HG_BLOCK = 16
SSM_HEADS = 16
SSM_HEAD_DIM = 64
SSM_WIDTH = SSM_HEADS * SSM_HEAD_DIM
SSM_GROUPS = 4
SSM_STATE = 128
SSM_CONV = 5
SSM_CONV_CH = SSM_WIDTH + 2 * SSM_GROUPS * SSM_STATE
SSM_CHUNK = 128
ATT_HEADS = 8
ATT_KV_HEADS = 2
ATT_HEAD_DIM = 128
ATT_GROUP = ATT_HEADS // ATT_KV_HEADS
ATT_WIDTH = ATT_HEADS * ATT_HEAD_DIM
ATT_KV_WIDTH = ATT_KV_HEADS * ATT_HEAD_DIM
ROPE_THETA = 10000.0
N_BRANCHES = 3
N_EXPERTS = 8
LANES = 128
SUBLANES = 8
VMEM_LIMIT = 56 * 1024 * 1024


def _params(*sem):
    return pltpu.CompilerParams(dimension_semantics=sem, vmem_limit_bytes=VMEM_LIMIT)


def _tile(n, prefs):
    for t in prefs:
        if n % t == 0:
            return t
    return n


def _sigmoid(x):
    return 1.0 / (1.0 + jnp.exp(-x))


def _silu(x):
    return x / (1.0 + jnp.exp(-x))


def _dot(a, b):
    return jnp.dot(a, b, preferred_element_type=F32)


def _dot_nt(a, b):
    return lax.dot_general(a, b, (((1,), (1,)), ((), ())), preferred_element_type=F32)


def _dot_tn(a, b):
    return lax.dot_general(a, b, (((0,), (0,)), ((), ())), preferred_element_type=F32)


def _ada_kernel(c_ref, w_ref, b_ref, o_ref):
    s = _silu(c_ref[...])
    o_ref[...] = jnp.dot(s, w_ref[...], preferred_element_type=F32,
                         precision=lax.Precision.HIGHEST) + b_ref[...]


def ada_modulation(cc, w, b):
    rows, d = cc.shape
    n = w.shape[1]
    tn = _tile(n, (1024, 512, 256, 128))
    return pl.pallas_call(
        _ada_kernel,
        out_shape=jax.ShapeDtypeStruct((rows, n), F32),
        grid=(n // tn,),
        in_specs=[pl.BlockSpec((rows, d), lambda j: (0, 0)),
                  pl.BlockSpec((d, tn), lambda j: (0, j)),
                  pl.BlockSpec((1, tn), lambda j: (0, j))],
        out_specs=pl.BlockSpec((rows, tn), lambda j: (0, j)),
        compiler_params=_params("arbitrary"),
        name="ada_modulation",
    )(cc, w, b.reshape(1, n))


def _normmod_kernel(x_ref, w_ref, sh_ref, sc_ref, o_ref):
    x = x_ref[0]
    ms = jnp.mean(x * x, axis=-1, keepdims=True)
    y = x * lax.rsqrt(ms + EPS) * w_ref[...]
    o_ref[0] = (y * (1.0 + sc_ref[0]) + sh_ref[0]).astype(o_ref.dtype)


def norm_modulate(x, w, shift, scale, out_dtype=BF16):
    b, t, d = x.shape
    tm = _tile(t, (512, 256, 128))
    bm = shift.shape[0]
    mod_map = (lambda i, j: (i, 0, 0)) if bm == b else (lambda i, j: (0, 0, 0))
    return pl.pallas_call(
        _normmod_kernel,
        out_shape=jax.ShapeDtypeStruct((b, t, d), out_dtype),
        grid=(b, t // tm),
        in_specs=[pl.BlockSpec((1, tm, d), lambda i, j: (i, j, 0)),
                  pl.BlockSpec((1, d), lambda i, j: (0, 0)),
                  pl.BlockSpec((1, 1, d), mod_map),
                  pl.BlockSpec((1, 1, d), mod_map)],
        out_specs=pl.BlockSpec((1, tm, d), lambda i, j: (i, j, 0)),
        compiler_params=_params("arbitrary", "arbitrary"),
        name="norm_modulate",
    )(x, w.reshape(1, d), shift, scale)


def _rmsnorm_kernel(x_ref, w_ref, o_ref):
    x = x_ref[...]
    ms = jnp.mean(x * x, axis=-1, keepdims=True)
    o_ref[...] = x * lax.rsqrt(ms + EPS) * w_ref[...]


def rms_norm_rows(x, w):
    m, d = x.shape
    tm = _tile(m, (512, 256, 128))
    return pl.pallas_call(
        _rmsnorm_kernel,
        out_shape=jax.ShapeDtypeStruct((m, d), F32),
        grid=(m // tm,),
        in_specs=[pl.BlockSpec((tm, d), lambda i: (i, 0)),
                  pl.BlockSpec((1, d), lambda i: (0, 0))],
        out_specs=pl.BlockSpec((tm, d), lambda i: (i, 0)),
        compiler_params=_params("arbitrary"),
        name="final_rms_norm",
    )(x, w.reshape(1, d))


def _mm_kernel(a_ref, w_ref, o_ref, *, act):
    acc = _dot(a_ref[...], w_ref[...])
    if act == "sigmoid":
        acc = _sigmoid(acc)
    o_ref[...] = acc.astype(o_ref.dtype)


def matmul(a, w, out_dtype, act=None):
    m, k = a.shape
    n = w.shape[1]
    tm = _tile(m, (1024, 512, 256, 128))
    tn = _tile(n, (1024, 512, 256, 128))
    return pl.pallas_call(
        functools.partial(_mm_kernel, act=act),
        out_shape=jax.ShapeDtypeStruct((m, n), out_dtype),
        grid=(m // tm, n // tn),
        in_specs=[pl.BlockSpec((tm, k), lambda i, j: (i, 0)),
                  pl.BlockSpec((k, tn), lambda i, j: (0, j))],
        out_specs=pl.BlockSpec((tm, tn), lambda i, j: (i, j)),
        compiler_params=_params("arbitrary", "arbitrary"),
        name="matmul_" + (act or "plain"),
    )(a, w)


def _mm_res_kernel(a_ref, w_ref, r_ref, m_ref, o_ref):
    o_ref[...] = r_ref[...] + m_ref[0] * _dot(a_ref[...], w_ref[...])


def matmul_residual(a, w, res, mod):
    m, k = a.shape
    n = w.shape[1]
    rows_per_mod = m // mod.shape[0]
    tm = _tile(rows_per_mod, (1024, 512, 256, 128))
    tn = _tile(n, (1024, 512, 256, 128))
    return pl.pallas_call(
        _mm_res_kernel,
        out_shape=jax.ShapeDtypeStruct((m, n), F32),
        grid=(m // tm, n // tn),
        in_specs=[pl.BlockSpec((tm, k), lambda i, j: (i, 0)),
                  pl.BlockSpec((k, tn), lambda i, j: (0, j)),
                  pl.BlockSpec((tm, tn), lambda i, j: (i, j)),
                  pl.BlockSpec((1, 1, tn), lambda i, j: (i * tm // rows_per_mod, 0, j))],
        out_specs=pl.BlockSpec((tm, tn), lambda i, j: (i, j)),
        compiler_params=_params("arbitrary", "arbitrary"),
        name="matmul_residual",
    )(a, w, res, mod)


def _merge_kernel(ya_ref, yb_ref, yc_ref, g0_ref, g1_ref, g2_ref, wb_ref, o_ref):
    acc = g0_ref[...].astype(F32) * _dot(ya_ref[...], wb_ref[0])
    acc = acc + g1_ref[...].astype(F32) * _dot(yb_ref[...], wb_ref[1])
    acc = acc + g2_ref[...].astype(F32) * _dot(yc_ref[...], wb_ref[2])
    o_ref[...] = acc.astype(o_ref.dtype)


def merge_branches(ya, yb, yc, gates, wb):
    m, kb = ya.shape
    d = wb.shape[2]
    tm = _tile(m, (1024, 512, 256, 128))
    tn = _tile(d, (512, 256, 128))
    nj = d // tn
    y_spec = pl.BlockSpec((tm, kb), lambda i, j: (i, 0))

    def g_spec(nb):
        return pl.BlockSpec((tm, tn), lambda i, j: (i, nb * nj + j))

    return pl.pallas_call(
        _merge_kernel,
        out_shape=jax.ShapeDtypeStruct((m, d), BF16),
        grid=(m // tm, nj),
        in_specs=[y_spec, y_spec, y_spec, g_spec(0), g_spec(1), g_spec(2),
                  pl.BlockSpec((N_BRANCHES, kb, tn), lambda i, j: (0, 0, j))],
        out_specs=pl.BlockSpec((tm, tn), lambda i, j: (i, j)),
        compiler_params=_params("arbitrary", "arbitrary"),
        name="merge_branches",
    )(ya, yb, yc, gates, gates, gates, wb)


def _ffn_kernel(h_ref, wg_ref, wu_ref, wd_ref, x_ref, m_ref, o_ref, acc_ref):
    f = pl.program_id(1)

    @pl.when(f == 0)
    def _():
        acc_ref[...] = jnp.zeros_like(acc_ref)

    h = h_ref[...]
    a = _silu(_dot(h, wg_ref[...])) * _dot(h, wu_ref[...])
    acc_ref[...] += _dot(a.astype(BF16), wd_ref[...])

    @pl.when(f == pl.num_programs(1) - 1)
    def _():
        o_ref[...] = x_ref[...] + m_ref[0] * acc_ref[...]


def swiglu_residual(h, wg, wu, wd, x, mod):
    m, d = h.shape
    ff = wg.shape[1]
    rows_per_mod = m // mod.shape[0]
    tm = _tile(rows_per_mod, (512, 256, 128))
    tf = _tile(ff, (512, 256, 128))
    return pl.pallas_call(
        _ffn_kernel,
        out_shape=jax.ShapeDtypeStruct((m, d), F32),
        grid=(m // tm, ff // tf),
        in_specs=[pl.BlockSpec((tm, d), lambda i, f: (i, 0)),
                  pl.BlockSpec((d, tf), lambda i, f: (0, f)),
                  pl.BlockSpec((d, tf), lambda i, f: (0, f)),
                  pl.BlockSpec((tf, d), lambda i, f: (f, 0)),
                  pl.BlockSpec((tm, d), lambda i, f: (i, 0)),
                  pl.BlockSpec((1, 1, d), lambda i, f: (i * tm // rows_per_mod, 0, 0))],
        out_specs=pl.BlockSpec((tm, d), lambda i, f: (i, 0)),
        scratch_shapes=[pltpu.VMEM((tm, d), F32)],
        compiler_params=_params("arbitrary", "arbitrary"),
        name="swiglu_residual",
    )(h, wg, wu, wd, x, mod)


def _router_kernel(x_ref, w_ref, sh_ref, sc_ref, rw_ref, h_ref, g_ref):
    x = x_ref[0]
    ms = jnp.mean(x * x, axis=-1, keepdims=True)
    h = x * lax.rsqrt(ms + EPS) * w_ref[...] * (1.0 + sc_ref[0]) + sh_ref[0]
    h_ref[0] = h.astype(h_ref.dtype)
    logits = jnp.dot(h, rw_ref[...], preferred_element_type=F32, precision=lax.Precision.HIGHEST)
    lane = lax.broadcasted_iota(jnp.int32, logits.shape, 1)
    logits = jnp.where(lane < N_EXPERTS, logits, -jnp.inf)
    m1 = jnp.max(logits, axis=-1, keepdims=True)
    i1 = jnp.min(jnp.where(logits == m1, lane, LANES), axis=-1, keepdims=True)
    rest = jnp.where(lane == i1, -jnp.inf, logits)
    m2 = jnp.max(rest, axis=-1, keepdims=True)
    i2 = jnp.min(jnp.where(rest == m2, lane, LANES), axis=-1, keepdims=True)
    e2 = jnp.exp(m2 - m1)
    den = 1.0 + e2
    g_ref[0] = jnp.where(lane == i1, 1.0 / den, 0.0) + jnp.where(lane == i2, e2 / den, 0.0)


def route_top2(x, w, shift, scale, router_w):
    b, t, d = x.shape
    tm = _tile(t, (512, 256, 128))
    rw = jnp.zeros((d, LANES), F32).at[:, :N_EXPERTS].set(router_w)
    return pl.pallas_call(
        _router_kernel,
        out_shape=(jax.ShapeDtypeStruct((b, t, d), BF16), jax.ShapeDtypeStruct((b, t, LANES), F32)),
        grid=(b, t // tm),
        in_specs=[pl.BlockSpec((1, tm, d), lambda i, j: (i, j, 0)),
                  pl.BlockSpec((1, d), lambda i, j: (0, 0)),
                  pl.BlockSpec((1, 1, d), lambda i, j: (i, 0, 0)),
                  pl.BlockSpec((1, 1, d), lambda i, j: (i, 0, 0)),
                  pl.BlockSpec((d, LANES), lambda i, j: (0, 0))],
        out_specs=(pl.BlockSpec((1, tm, d), lambda i, j: (i, j, 0)),
                   pl.BlockSpec((1, tm, LANES), lambda i, j: (i, j, 0))),
        compiler_params=_params("arbitrary", "arbitrary"),
        name="moe_router",
    )(x, w.reshape(1, d), shift, scale, rw)


def _moe_kernel(h_ref, g_ref, wg_ref, wu_ref, wd_ref, x_ref, m_ref, o_ref, acc_ref):
    e = pl.program_id(1)
    f = pl.program_id(2)

    @pl.when((e == 0) & (f == 0))
    def _():
        acc_ref[...] = jnp.zeros_like(acc_ref)

    h = h_ref[...]
    gates = g_ref[...]
    lane = lax.broadcasted_iota(jnp.int32, gates.shape, 1)
    gate = jnp.sum(jnp.where(lane == e, gates, 0.0), axis=-1, keepdims=True)
    a = _silu(_dot(h, wg_ref[0])) * _dot(h, wu_ref[0])
    acc_ref[...] += gate * _dot(a.astype(BF16), wd_ref[0])

    @pl.when((e == pl.num_programs(1) - 1) & (f == pl.num_programs(2) - 1))
    def _():
        o_ref[...] = x_ref[...] + m_ref[0] * acc_ref[...]


def moe_swiglu_residual(h, gates, wg, wu, wd, x, mod):
    m, d = h.shape
    ne, _, ff = wg.shape
    rows_per_mod = m // mod.shape[0]
    tm = _tile(rows_per_mod, (512, 256, 128))
    tf = _tile(ff, (512, 256, 128))
    return pl.pallas_call(
        _moe_kernel,
        out_shape=jax.ShapeDtypeStruct((m, d), F32),
        grid=(m // tm, ne, ff // tf),
        in_specs=[pl.BlockSpec((tm, d), lambda i, e, f: (i, 0)),
                  pl.BlockSpec((tm, LANES), lambda i, e, f: (i, 0)),
                  pl.BlockSpec((1, d, tf), lambda i, e, f: (e, 0, f)),
                  pl.BlockSpec((1, d, tf), lambda i, e, f: (e, 0, f)),
                  pl.BlockSpec((1, tf, d), lambda i, e, f: (e, f, 0)),
                  pl.BlockSpec((tm, d), lambda i, e, f: (i, 0)),
                  pl.BlockSpec((1, 1, d), lambda i, e, f: (i * tm // rows_per_mod, 0, 0))],
        out_specs=pl.BlockSpec((tm, d), lambda i, e, f: (i, 0)),
        scratch_shapes=[pltpu.VMEM((tm, d), F32)],
        compiler_params=_params("arbitrary", "arbitrary", "arbitrary"),
        name="moe_swiglu_residual",
    )(h, gates, wg, wu, wd, x, mod)


def _hgrn2_kernel(q_ref, f_ref, v_ref, lb_ref, s0_ref, o_ref, sout_ref, st_ref, c_ref, k_ref,
                  *, reverse, rows):
    j = pl.program_id(1)

    @pl.when(j == 0)
    def _():
        st_ref[...] = s0_ref[0]

    lb = lb_ref[...]
    fr = f_ref[0]
    log_sig = jnp.minimum(fr, 0.0) - jnp.log1p(jnp.exp(-jnp.abs(fr)))
    a = jnp.log(lb)
    b = jnp.log1p(-lb) + log_sig
    log_f = jnp.maximum(a, b) + jnp.log1p(jnp.exp(-jnp.abs(a - b)))
    k_ref[...] = (1.0 - lb) / (1.0 + jnp.exp(fr))
    rib = lax.broadcasted_iota(jnp.int32, fr.shape, 0) & (HG_BLOCK - 1)
    c = log_f
    sh = 1
    while sh < HG_BLOCK:
        if reverse:
            c = c + jnp.where(rib < HG_BLOCK - sh, pltpu.roll(c, rows - sh, axis=0), 0.0)
        else:
            c = c + jnp.where(rib >= sh, pltpu.roll(c, sh, axis=0), 0.0)
        sh *= 2
    c_ref[...] = c

    trow = lax.broadcasted_iota(jnp.int32, (HG_BLOCK, HG_DK), 0)
    blocks = list(range(rows // HG_BLOCK))
    if reverse:
        blocks = blocks[::-1]

    def head(h, carry):
        off = pl.multiple_of(h * HG_DK, HG_DK)
        s_t = st_ref[h]
        for blk in blocks:
            r0 = blk * HG_BLOCK
            qb = q_ref[0, r0:r0 + HG_BLOCK, pl.ds(off, HG_DK)] * (HG_DK ** -0.5)
            vb = v_ref[0, r0:r0 + HG_BLOCK, pl.ds(off, HG_DK)]
            kb = k_ref[r0:r0 + HG_BLOCK, pl.ds(off, HG_DK)]
            cb = c_ref[r0:r0 + HG_BLOCK, pl.ds(off, HG_DK)]
            o = _dot_nt((qb * jnp.exp(cb)).astype(BF16), s_t.astype(BF16))
            for jj in range(HG_BLOCK):
                e = jnp.exp(jnp.minimum(cb - cb[jj:jj + 1], 0.0)) * (qb * kb[jj:jj + 1])
                col = jnp.sum(e, axis=-1, keepdims=True)
                valid = (trow <= jj) if reverse else (trow >= jj)
                o = o + jnp.where(valid, col, 0.0) * vb[jj:jj + 1]
            c_tot = cb[0:1] if reverse else cb[HG_BLOCK - 1:HG_BLOCK]
            kt = (kb * jnp.exp(c_tot - cb)).astype(BF16)
            s_t = s_t * jnp.exp(c_tot) + _dot_tn(vb.astype(BF16), kt)
            o_ref[0, r0:r0 + HG_BLOCK, pl.ds(off, HG_DK)] = o
        st_ref[h] = s_t
        return carry

    lax.fori_loop(0, HG_HEADS, head, 0)

    @pl.when(j == pl.num_programs(1) - 1)
    def _():
        sout_ref[0] = st_ref[...]


def hgrn2_scan(p_hg, lb, s0, reverse):
    b, t, _ = p_hg.shape
    w = HG_WIDTH
    rows = _tile(t, (64, 32, 16))
    nt = t // rows
    tmap = (lambda j: nt - 1 - j) if reverse else (lambda j: j)
    fcol = 2 if reverse else 1
    return pl.pallas_call(
        functools.partial(_hgrn2_kernel, reverse=reverse, rows=rows),
        out_shape=(jax.ShapeDtypeStruct((b, t, w), F32),
                   jax.ShapeDtypeStruct(s0.shape, F32)),
        grid=(b, nt),
        in_specs=[pl.BlockSpec((1, rows, w), lambda i, j: (i, tmap(j), 0)),
                  pl.BlockSpec((1, rows, w), lambda i, j: (i, tmap(j), fcol)),
                  pl.BlockSpec((1, rows, w), lambda i, j: (i, tmap(j), 3)),
                  pl.BlockSpec((1, w), lambda i, j: (0, 0)),
                  pl.BlockSpec((1, HG_HEADS, HG_DK, HG_DK), lambda i, j: (i, 0, 0, 0))],
        out_specs=(pl.BlockSpec((1, rows, w), lambda i, j: (i, tmap(j), 0)),
                   pl.BlockSpec((1, HG_HEADS, HG_DK, HG_DK), lambda i, j: (i, 0, 0, 0))),
        scratch_shapes=[pltpu.VMEM((HG_HEADS, HG_DK, HG_DK), F32),
                        pltpu.VMEM((rows, w), F32),
                        pltpu.VMEM((rows, w), F32)],
        compiler_params=_params("arbitrary", "arbitrary"),
        name="hgrn2_scan_bwd" if reverse else "hgrn2_scan_fwd",
    )(p_hg, p_hg, p_hg, lb.reshape(1, w), s0)


def _hg_readout_kernel(of_ref, ob_ref, g_ref, w_ref, y_ref):
    for h in range(HG_HEADS):
        sl = slice(h * HG_DK, (h + 1) * HG_DK)
        o = of_ref[:, sl] + ob_ref[:, sl]
        ms = jnp.mean(o * o, axis=-1, keepdims=True)
        y = o * lax.rsqrt(ms + EPS) * w_ref[:, sl]
        y_ref[:, sl] = (y * _silu(g_ref[:, sl])).astype(y_ref.dtype)


def hgrn2_readout(o_f, o_b, p_hg, norm_w):
    m, w = o_f.shape
    tm = _tile(m, (512, 256, 128))
    spec = pl.BlockSpec((tm, w), lambda i: (i, 0))
    return pl.pallas_call(
        _hg_readout_kernel,
        out_shape=jax.ShapeDtypeStruct((m, w), BF16),
        grid=(m // tm,),
        in_specs=[spec, spec, pl.BlockSpec((tm, w), lambda i: (i, 4)),
                  pl.BlockSpec((1, w), lambda i: (0, 0))],
        out_specs=spec,
        compiler_params=_params("arbitrary"),
        name="hgrn2_readout",
    )(o_f, o_b, p_hg, norm_w.reshape(1, w))


def _conv_kernel(prev_ref, cur_ref, next_ref, w_ref, b_ref, o_ref, *, tm):
    i = pl.program_id(1)
    prev = jnp.where(i > 0, prev_ref[0], 0.0)
    nxt = jnp.where(i < pl.num_programs(1) - 1, next_ref[0], 0.0)
    xe = jnp.concatenate([prev, cur_ref[0], nxt], axis=0)
    ext = tm + 2 * SUBLANES
    acc = jnp.zeros((tm, xe.shape[1]), F32) + b_ref[...]
    for k in range(SSM_CONV):
        sh = (SSM_CONV // 2 - k) % ext
        xs = xe if sh == 0 else pltpu.roll(xe, sh, axis=0)
        acc = acc + w_ref[k:k + 1, :] * xs[SUBLANES:SUBLANES + tm]
    o_ref[0] = _silu(acc)


def ssd_conv_silu(p_ssd, conv_w, conv_b):
    b, t, _ = p_ssd.shape
    ch = SSM_CONV_CH
    tm = _tile(t, (512, 256, 128))
    tc = 1024
    c0 = SSM_WIDTH // tc
    hb = tm // SUBLANES
    nh = t // SUBLANES
    return pl.pallas_call(
        functools.partial(_conv_kernel, tm=tm),
        out_shape=jax.ShapeDtypeStruct((b, t, ch), F32),
        grid=(b, t // tm, ch // tc),
        in_specs=[pl.BlockSpec((1, SUBLANES, tc), lambda i, j, cj: (i, jnp.maximum(j * hb - 1, 0), c0 + cj)),
                  pl.BlockSpec((1, tm, tc), lambda i, j, cj: (i, j, c0 + cj)),
                  pl.BlockSpec((1, SUBLANES, tc), lambda i, j, cj: (i, jnp.minimum((j + 1) * hb, nh - 1), c0 + cj)),
                  pl.BlockSpec((SUBLANES, tc), lambda i, j, cj: (0, cj)),
                  pl.BlockSpec((1, tc), lambda i, j, cj: (0, cj))],
        out_specs=pl.BlockSpec((1, tm, tc), lambda i, j, cj: (i, j, cj)),
        compiler_params=_params("arbitrary", "arbitrary", "arbitrary"),
        name="ssd_conv_silu",
    )(p_ssd, p_ssd, p_ssd,
      jnp.zeros((SUBLANES, ch), F32).at[:SSM_CONV].set(conv_w), conv_b.reshape(1, ch))


def _ssd_kernel(x_ref, b_ref, c_ref, dt_ref, prm_ref, h0_ref, y_ref, hout_ref, hst_ref, *, reverse):
    j = pl.program_id(1)
    cl = SSM_CHUNK

    @pl.when(j == 0)
    def _():
        hst_ref[...] = h0_ref[0]

    prm = prm_ref[...]
    dtr = dt_ref[0] + prm[0:1]
    dt = jnp.maximum(dtr, 0.0) + jnp.log1p(jnp.exp(-jnp.abs(dtr)))
    a = dt * prm[1:2]
    row = lax.broadcasted_iota(jnp.int32, a.shape, 0)
    ac = a
    sh = 1
    while sh < cl:
        if reverse:
            ac = ac + jnp.where(row < cl - sh, pltpu.roll(ac, cl - sh, axis=0), 0.0)
        else:
            ac = ac + jnp.where(row >= sh, pltpu.roll(ac, sh, axis=0), 0.0)
        sh *= 2
    ac_t = ac.T
    ti = lax.broadcasted_iota(jnp.int32, (cl, cl), 0)
    si = lax.broadcasted_iota(jnp.int32, (cl, cl), 1)
    mask = (ti <= si) if reverse else (ti >= si)
    lane0 = SSM_HEADS if reverse else 0
    heads_per_group = SSM_HEADS // SSM_GROUPS
    for g in range(SSM_GROUPS):
        bg = b_ref[0, :, g * SSM_STATE:(g + 1) * SSM_STATE].astype(BF16)
        cg = c_ref[0, :, g * SSM_STATE:(g + 1) * SSM_STATE].astype(BF16)
        cb = _dot_nt(cg, bg)
        for r in range(heads_per_group):
            h = g * heads_per_group + r
            hl = lane0 + h
            acol = ac[:, hl:hl + 1]
            arow = ac_t[hl:hl + 1, :]
            decay = jnp.where(mask, jnp.exp(jnp.minimum(acol - arow, 0.0)), 0.0)
            xh = x_ref[0, :, h * SSM_HEAD_DIM:(h + 1) * SSM_HEAD_DIM]
            xdt = xh * dt[:, hl:hl + 1]
            hh = hst_ref[h]
            y = _dot((cb * decay).astype(BF16), xdt.astype(BF16))
            y = y + jnp.exp(acol) * _dot_nt(cg, hh.astype(BF16))
            y = y + prm[2:3, hl:hl + 1] * xh
            tot = arow[:, 0:1] if reverse else arow[:, cl - 1:cl]
            xdec = (xdt * jnp.exp(tot - acol)).astype(BF16)
            hst_ref[h] = hh * jnp.exp(tot) + _dot_tn(xdec, bg)
            y_ref[0, :, h * SSM_HEAD_DIM:(h + 1) * SSM_HEAD_DIM] = y

    @pl.when(j == pl.num_programs(1) - 1)
    def _():
        hout_ref[0] = hst_ref[...]


def ssd_scan(xbc, dt_raw, prm, h0, reverse):
    b, t, _ = xbc.shape
    cl = SSM_CHUNK
    nt = t // cl
    gw = SSM_GROUPS * SSM_STATE
    tmap = (lambda j: nt - 1 - j) if reverse else (lambda j: j)
    return pl.pallas_call(
        functools.partial(_ssd_kernel, reverse=reverse),
        out_shape=(jax.ShapeDtypeStruct((b, t, SSM_WIDTH), F32),
                   jax.ShapeDtypeStruct(h0.shape, F32)),
        grid=(b, nt),
        in_specs=[pl.BlockSpec((1, cl, SSM_WIDTH), lambda i, j: (i, tmap(j), 0)),
                  pl.BlockSpec((1, cl, gw), lambda i, j: (i, tmap(j), SSM_WIDTH // gw)),
                  pl.BlockSpec((1, cl, gw), lambda i, j: (i, tmap(j), SSM_WIDTH // gw + 1)),
                  pl.BlockSpec((1, cl, LANES), lambda i, j: (i, tmap(j), 0)),
                  pl.BlockSpec((SUBLANES, LANES), lambda i, j: (0, 0)),
                  pl.BlockSpec((1, SSM_HEADS, SSM_HEAD_DIM, SSM_STATE), lambda i, j: (i, 0, 0, 0))],
        out_specs=(pl.BlockSpec((1, cl, SSM_WIDTH), lambda i, j: (i, tmap(j), 0)),
                   pl.BlockSpec((1, SSM_HEADS, SSM_HEAD_DIM, SSM_STATE), lambda i, j: (i, 0, 0, 0))),
        scratch_shapes=[pltpu.VMEM((SSM_HEADS, SSM_HEAD_DIM, SSM_STATE), F32)],
        compiler_params=_params("arbitrary", "arbitrary"),
        name="ssd_scan_bwd" if reverse else "ssd_scan_fwd",
    )(xbc, xbc, xbc, dt_raw, prm, h0)


def _ssd_readout_kernel(yf_ref, yb_ref, z_ref, w_ref, o_ref):
    gw = SSM_WIDTH // SSM_GROUPS
    for g in range(SSM_GROUPS):
        sl = slice(g * gw, (g + 1) * gw)
        y = (yf_ref[:, sl] + yb_ref[:, sl]) * _silu(z_ref[:, sl])
        ms = jnp.mean(y * y, axis=-1, keepdims=True)
        o_ref[:, sl] = (y * lax.rsqrt(ms + EPS) * w_ref[:, sl]).astype(o_ref.dtype)


def ssd_readout(y_f, y_b, p_ssd, norm_w):
    m, w = y_f.shape
    tm = _tile(m, (512, 256, 128))
    spec = pl.BlockSpec((tm, w), lambda i: (i, 0))
    return pl.pallas_call(
        _ssd_readout_kernel,
        out_shape=jax.ShapeDtypeStruct((m, w), BF16),
        grid=(m // tm,),
        in_specs=[spec, spec, spec, pl.BlockSpec((1, w), lambda i: (0, 0))],
        out_specs=spec,
        compiler_params=_params("arbitrary"),
        name="ssd_readout",
    )(y_f, y_b, p_ssd, norm_w.reshape(1, w))


def _attprep_kernel(p_ref, qw_ref, kw_ref, cos_ref, sin_ref, q_ref, k_ref, v_ref, *, rope):
    lane = lax.broadcasted_iota(jnp.int32, (p_ref.shape[0], ATT_HEAD_DIM), 1)
    first = (lane % (ATT_HEAD_DIM // 2)) < (ATT_HEAD_DIM // 4)
    for h in range(ATT_HEADS + ATT_KV_HEADS):
        xh = p_ref[:, h * ATT_HEAD_DIM:(h + 1) * ATT_HEAD_DIM]
        w = qw_ref[...] if h < ATT_HEADS else kw_ref[...]
        ms = jnp.mean(xh * xh, axis=-1, keepdims=True)
        y = xh * lax.rsqrt(ms + EPS) * w
        if rope:
            partner = jnp.where(first, pltpu.roll(y, 3 * ATT_HEAD_DIM // 4, axis=1),
                                pltpu.roll(y, ATT_HEAD_DIM // 4, axis=1))
            y = y * cos_ref[...] + partner * sin_ref[...]
        if h < ATT_HEADS:
            q_ref[:, h * ATT_HEAD_DIM:(h + 1) * ATT_HEAD_DIM] = (y * ATT_HEAD_DIM ** -0.5).astype(q_ref.dtype)
        else:
            hk = h - ATT_HEADS
            k_ref[:, hk * ATT_HEAD_DIM:(hk + 1) * ATT_HEAD_DIM] = y.astype(k_ref.dtype)
    v_ref[...] = p_ref[:, ATT_WIDTH + ATT_KV_WIDTH:].astype(v_ref.dtype)


def attention_prep(p_att, q_norm_w, k_norm_w, cos, sin, seq, rope):
    m, wtot = p_att.shape
    tm = _tile(seq, (512, 256, 128))
    npos = seq // tm
    tab = pl.BlockSpec((tm, ATT_HEAD_DIM), (lambda i: (i % npos, 0)) if rope else (lambda i: (0, 0)))
    wspec = pl.BlockSpec((1, ATT_HEAD_DIM), lambda i: (0, 0))
    return pl.pallas_call(
        functools.partial(_attprep_kernel, rope=rope),
        out_shape=(jax.ShapeDtypeStruct((m, ATT_WIDTH), BF16),
                   jax.ShapeDtypeStruct((m, ATT_KV_WIDTH), BF16),
                   jax.ShapeDtypeStruct((m, ATT_KV_WIDTH), BF16)),
        grid=(m // tm,),
        in_specs=[pl.BlockSpec((tm, wtot), lambda i: (i, 0)), wspec, wspec, tab, tab],
        out_specs=(pl.BlockSpec((tm, ATT_WIDTH), lambda i: (i, 0)),
                   pl.BlockSpec((tm, ATT_KV_WIDTH), lambda i: (i, 0)),
                   pl.BlockSpec((tm, ATT_KV_WIDTH), lambda i: (i, 0))),
        compiler_params=_params("arbitrary"),
        name="attention_prep_rope" if rope else "attention_prep",
    )(p_att, q_norm_w.reshape(1, -1), k_norm_w.reshape(1, -1), cos, sin)


def _attn_kernel(q_ref, k_ref, v_ref, o_ref, *, tk, nk):
    tq = q_ref.shape[1]
    q = q_ref[0]
    qs = jnp.concatenate([q[:, g * ATT_HEAD_DIM:(g + 1) * ATT_HEAD_DIM] for g in range(ATT_GROUP)], axis=0)

    def body(c, carry):
        m, l, acc = carry
        off = pl.multiple_of(c * tk, tk)
        kc = k_ref[0, pl.ds(off, tk), :]
        vc = v_ref[0, pl.ds(off, tk), :]
        s = _dot_nt(qs, kc)
        mn = jnp.maximum(m, jnp.max(s, axis=-1, keepdims=True))
        alpha = jnp.exp(m - mn)
        p = jnp.exp(s - mn)
        l = alpha * l + jnp.sum(p, axis=-1, keepdims=True)
        acc = alpha * acc + _dot(p.astype(BF16), vc)
        return mn, l, acc

    rows = ATT_GROUP * tq
    init = (jnp.full((rows, 1), -jnp.inf, F32), jnp.zeros((rows, 1), F32),
            jnp.zeros((rows, ATT_HEAD_DIM), F32))
    _, l, acc = lax.fori_loop(0, nk, body, init)
    o = acc / l
    o_ref[0] = jnp.concatenate([o[g * tq:(g + 1) * tq] for g in range(ATT_GROUP)], axis=1).astype(o_ref.dtype)


def gqa_attention(q, k, v):
    b, t, _ = q.shape
    s = k.shape[1]
    tq = _tile(t, (256, 128))
    tk = s if s <= 1024 else _tile(s, (768, 512, 384, 256, 128))
    gwid = ATT_GROUP * ATT_HEAD_DIM
    return pl.pallas_call(
        functools.partial(_attn_kernel, tk=tk, nk=s // tk),
        out_shape=jax.ShapeDtypeStruct(q.shape, BF16),
        grid=(b, ATT_KV_HEADS, t // tq),
        in_specs=[pl.BlockSpec((1, tq, gwid), lambda i, h, j: (i, j, h)),
                  pl.BlockSpec((1, s, ATT_HEAD_DIM), lambda i, h, j: (i, 0, h)),
                  pl.BlockSpec((1, s, ATT_HEAD_DIM), lambda i, h, j: (i, 0, h))],
        out_specs=pl.BlockSpec((1, tq, gwid), lambda i, h, j: (i, j, h)),
        compiler_params=_params("arbitrary", "arbitrary", "arbitrary"),
        name="gqa_attention",
    )(q, k, v)


def _rope_tables(n_tokens):
    rows = n_tokens // GRID_W
    row = jnp.repeat(jnp.arange(rows, dtype=F32), GRID_W)
    col = jnp.tile(jnp.arange(GRID_W, dtype=F32), rows)
    axis_dim = ATT_HEAD_DIM // 2
    inv = ROPE_THETA ** (-jnp.arange(0, axis_dim, 2, dtype=F32) / axis_dim)
    ar, ac = row[:, None] * inv, col[:, None] * inv
    cos = jnp.concatenate([jnp.cos(ar), jnp.cos(ar), jnp.cos(ac), jnp.cos(ac)], axis=-1)
    sin = jnp.concatenate([-jnp.sin(ar), jnp.sin(ar), -jnp.sin(ac), jnp.sin(ac)], axis=-1)
    return cos, sin


def kernel(x, c, ctx, c_ctx, ada_w, ada_b, norm_mix_w, norm_ffn_w, w_in, hg_lb_logits, hg_norm_w, ssm_conv_w, ssm_conv_b, ssm_dt_bias, ssm_a_log, ssm_d, ssm_norm_w, attn_q_norm_w, attn_k_norm_w, w_branch, w_out, ffn_w_gate, ffn_w_up, ffn_w_down, moe_router, moe_w_gate, moe_w_up, moe_w_down, final_norm_w):
    bsz, seq, d = x.shape
    clen = ctx.shape[1]
    depth = ada_w.shape[0]
    m_lat, m_ctx = bsz * seq, bsz * clen

    lb_all = jnp.cumsum(jax.nn.softmax(hg_lb_logits.astype(F32), axis=0), axis=0)
    lb_all = lb_all - lb_all[0]
    cond = jnp.zeros((SUBLANES, d), F32).at[:bsz].set(c).at[bsz].set(c_ctx)
    cos, sin = _rope_tables(seq)

    e_hg = 5 * HG_WIDTH
    e_ssd = e_hg + SSM_WIDTH + SSM_CONV_CH
    e_dt = e_ssd + 2 * SSM_HEADS
    e_att = e_dt + ATT_WIDTH + 2 * ATT_KV_WIDTH

    zeros_hg = jnp.zeros((bsz, HG_HEADS, HG_DK, HG_DK), F32)
    zeros_ssd = jnp.zeros((bsz, SSM_HEADS, SSM_HEAD_DIM, SSM_STATE), F32)

    x2 = x.reshape(m_lat, d)
    xc2 = ctx.reshape(m_ctx, d)
    for l in range(depth):
        need_ctx = l < depth - 1
        mod = ada_modulation(cond, ada_w[l], ada_b[l])
        mod_lat = mod[:bsz].reshape(bsz, 6, 1, d)
        mod_ctx = mod[bsz:bsz + 1].reshape(1, 6, 1, d)
        ml = [mod_lat[:, n] for n in range(6)]
        mc = [mod_ctx[:, n] for n in range(6)]

        wl = w_in[l]
        w_hg = wl[:, :e_hg].astype(BF16)
        w_ssd = wl[:, e_hg:e_ssd].astype(BF16)
        w_dt = jnp.zeros((d, LANES), BF16).at[:, :2 * SSM_HEADS].set(wl[:, e_ssd:e_dt].astype(BF16))
        w_att = wl[:, e_dt:e_att].astype(BF16)
        w_gate = wl[:, e_att:].astype(BF16)
        w_br = w_branch[l].astype(BF16)
        w_o = w_out[l].astype(BF16)

        prm = jnp.zeros((SUBLANES, LANES), F32)
        prm = prm.at[0, :2 * SSM_HEADS].set(ssm_dt_bias[l].reshape(-1).astype(F32))
        prm = prm.at[1, :2 * SSM_HEADS].set(-jnp.exp(ssm_a_log[l].reshape(-1).astype(F32)))
        prm = prm.at[2, :2 * SSM_HEADS].set(ssm_d[l].reshape(-1).astype(F32))

        def mixer_inputs(xs, nb, t, shift, scale):
            h = norm_modulate(xs.reshape(nb, t, d), norm_mix_w[l], shift, scale).reshape(nb * t, d)
            p_hg = matmul(h, w_hg, F32).reshape(nb, t, e_hg)
            p_ssd = matmul(h, w_ssd, F32).reshape(nb, t, e_ssd - e_hg)
            p_dt = matmul(h, w_dt, F32).reshape(nb, t, LANES)
            p_att = matmul(h, w_att, F32)
            xbc = ssd_conv_silu(p_ssd, ssm_conv_w[l], ssm_conv_b[l])
            return h, p_hg, p_ssd, p_dt, p_att, xbc

        h_c, phg_c, pssd_c, pdt_c, patt_c, xbc_c = mixer_inputs(xc2, bsz, clen, mc[0], mc[1])
        h_l, phg_l, pssd_l, pdt_l, patt_l, xbc_l = mixer_inputs(x2, bsz, seq, ml[0], ml[1])

        ofc, s_f = hgrn2_scan(phg_c, lb_all[l, 0], zeros_hg, False)
        obc, s_b = hgrn2_scan(phg_c, lb_all[l, 1], zeros_hg, True)
        ofl, _ = hgrn2_scan(phg_l, lb_all[l, 0], s_f, False)
        obl, _ = hgrn2_scan(phg_l, lb_all[l, 1], s_b, True)
        ya_l = hgrn2_readout(ofl.reshape(m_lat, -1), obl.reshape(m_lat, -1), phg_l.reshape(m_lat, -1), hg_norm_w[l])

        yfc, h_f = ssd_scan(xbc_c, pdt_c, prm, zeros_ssd, False)
        ybc, h_b = ssd_scan(xbc_c, pdt_c, prm, zeros_ssd, True)
        yfl, _ = ssd_scan(xbc_l, pdt_l, prm, h_f, False)
        ybl, _ = ssd_scan(xbc_l, pdt_l, prm, h_b, True)
        yb_l = ssd_readout(yfl.reshape(m_lat, -1), ybl.reshape(m_lat, -1), pssd_l.reshape(m_lat, -1), ssm_norm_w[l])

        q_c, k_c, v_c = attention_prep(patt_c, attn_q_norm_w[l], attn_k_norm_w[l], cos, sin, clen, False)
        q_l, k_l, v_l = attention_prep(patt_l, attn_q_norm_w[l], attn_k_norm_w[l], cos, sin, seq, True)
        k_c3, v_c3 = k_c.reshape(bsz, clen, -1), v_c.reshape(bsz, clen, -1)
        k_all = jnp.concatenate([k_c3, k_l.reshape(bsz, seq, -1)], axis=1)
        v_all = jnp.concatenate([v_c3, v_l.reshape(bsz, seq, -1)], axis=1)
        yc_l = gqa_attention(q_l.reshape(bsz, seq, -1), k_all, v_all).reshape(m_lat, -1)

        gates_l = matmul(h_l, w_gate, BF16, act="sigmoid")
        merged_l = merge_branches(ya_l, yb_l, yc_l, gates_l, w_br)
        x2 = matmul_residual(merged_l, w_o, x2, ml[2])

        if need_ctx:
            ya_c = hgrn2_readout(ofc.reshape(m_ctx, -1), obc.reshape(m_ctx, -1), phg_c.reshape(m_ctx, -1), hg_norm_w[l])
            yb_c = ssd_readout(yfc.reshape(m_ctx, -1), ybc.reshape(m_ctx, -1), pssd_c.reshape(m_ctx, -1), ssm_norm_w[l])
            yc_c = gqa_attention(q_c.reshape(bsz, clen, -1), k_c3, v_c3).reshape(m_ctx, -1)
            gates_c = matmul(h_c, w_gate, BF16, act="sigmoid")
            merged_c = merge_branches(ya_c, yb_c, yc_c, gates_c, w_br)
            xc2 = matmul_residual(merged_c, w_o, xc2, mc[2])

        if l % 2 == 0:
            wg = ffn_w_gate[l // 2].astype(BF16)
            wu = ffn_w_up[l // 2].astype(BF16)
            wd = ffn_w_down[l // 2].astype(BF16)
            hf = norm_modulate(x2.reshape(bsz, seq, d), norm_ffn_w[l], ml[3], ml[4]).reshape(m_lat, d)
            x2 = swiglu_residual(hf, wg, wu, wd, x2, ml[5])
            if need_ctx:
                hfc = norm_modulate(xc2.reshape(bsz, clen, d), norm_ffn_w[l], mc[3], mc[4]).reshape(m_ctx, d)
                xc2 = swiglu_residual(hfc, wg, wu, wd, xc2, mc[5])
        else:
            wg = moe_w_gate[l // 2].astype(BF16)
            wu = moe_w_up[l // 2].astype(BF16)
            wd = moe_w_down[l // 2].astype(BF16)
            hf, gates = route_top2(x2.reshape(bsz, seq, d), norm_ffn_w[l], ml[3], ml[4], moe_router[l // 2])
            x2 = moe_swiglu_residual(hf.reshape(m_lat, d), gates.reshape(m_lat, LANES), wg, wu, wd, x2, ml[5])
            if need_ctx:
                hfc, gates_c = route_top2(xc2.reshape(bsz, clen, d), norm_ffn_w[l],
                                          jnp.broadcast_to(mc[3], (bsz, 1, d)), jnp.broadcast_to(mc[4], (bsz, 1, d)),
                                          moe_router[l // 2])
                xc2 = moe_swiglu_residual(hfc.reshape(m_ctx, d), gates_c.reshape(m_ctx, LANES), wg, wu, wd, xc2, mc[5])

    return rms_norm_rows(x2, final_norm_w).reshape(bsz, seq, d)
```

```python
import functools
import math

import jax
import jax.numpy as jnp
from jax import lax
from jax.experimental import pallas as pl
from jax.experimental.pallas import tpu as pltpu

F32 = jnp.float32
BF16 = jnp.bfloat16

EPS = 1e-6
GRID_W = 64
HG_HEADS = 8
HG_DK = 128
HG_WIDTH = HG_HEADS * HG_DK
HG_BLOCK = 16
SSM_HEADS = 16
SSM_HEAD_DIM = 64
SSM_WIDTH = SSM_HEADS * SSM_HEAD_DIM
SSM_GROUPS = 4
SSM_STATE = 128
SSM_CONV = 5
SSM_CONV_CH = SSM_WIDTH + 2 * SSM_GROUPS * SSM_STATE
SSM_CHUNK = 128
ATT_HEADS = 8
ATT_KV_HEADS = 2
ATT_HEAD_DIM = 128
ATT_GROUP = ATT_HEADS // ATT_KV_HEADS
ATT_WIDTH = ATT_HEADS * ATT_HEAD_DIM
ATT_KV_WIDTH = ATT_KV_HEADS * ATT_HEAD_DIM
ATT_Q_SCALE = ATT_HEAD_DIM ** -0.5 * math.log2(math.e)
ROPE_THETA = 10000.0
N_BRANCHES = 3
N_EXPERTS = 8
LANES = 128
SUBLANES = 8
VMEM_LIMIT = 56 * 1024 * 1024


def _params(*sem):
    return pltpu.CompilerParams(dimension_semantics=sem, vmem_limit_bytes=VMEM_LIMIT)


def _tile(n, prefs):
    for t in prefs:
        if n % t == 0:
            return t
    return n


def _sigmoid(x):
    return 1.0 / (1.0 + jnp.exp(-x))


def _silu(x):
    return x / (1.0 + jnp.exp(-x))


def _dot(a, b):
    return jnp.dot(a, b, preferred_element_type=F32)


def _dot_nt(a, b):
    return lax.dot_general(a, b, (((1,), (1,)), ((), ())), preferred_element_type=F32)


def _dot_tn(a, b):
    return lax.dot_general(a, b, (((0,), (0,)), ((), ())), preferred_element_type=F32)


def _ada_kernel(c_ref, w_ref, b_ref, o_ref):
    s = _silu(c_ref[...])
    o_ref[...] = jnp.dot(s, w_ref[...], preferred_element_type=F32,
                         precision=lax.Precision.HIGHEST) + b_ref[...]


def ada_modulation(cc, w, b):
    rows, d = cc.shape
    n = w.shape[1]
    tn = _tile(n, (1024, 512, 256, 128))
    return pl.pallas_call(
        _ada_kernel,
        out_shape=jax.ShapeDtypeStruct((rows, n), F32),
        grid=(n // tn,),
        in_specs=[pl.BlockSpec((rows, d), lambda j: (0, 0)),
                  pl.BlockSpec((d, tn), lambda j: (0, j)),
                  pl.BlockSpec((1, tn), lambda j: (0, j))],
        out_specs=pl.BlockSpec((rows, tn), lambda j: (0, j)),
        compiler_params=_params("arbitrary"),
        name="ada_modulation",
    )(cc, w, b.reshape(1, n))


def _normmod_kernel(x_ref, w_ref, sh_ref, sc_ref, o_ref):
    x = x_ref[0]
    ms = jnp.mean(x * x, axis=-1, keepdims=True)
    y = x * lax.rsqrt(ms + EPS) * w_ref[...]
    o_ref[0] = (y * (1.0 + sc_ref[0]) + sh_ref[0]).astype(o_ref.dtype)


def norm_modulate(x, w, shift, scale, out_dtype=BF16):
    b, t, d = x.shape
    tm = _tile(t, (512, 256, 128))
    bm = shift.shape[0]
    mod_map = (lambda i, j: (i, 0, 0)) if bm == b else (lambda i, j: (0, 0, 0))
    return pl.pallas_call(
        _normmod_kernel,
        out_shape=jax.ShapeDtypeStruct((b, t, d), out_dtype),
        grid=(b, t // tm),
        in_specs=[pl.BlockSpec((1, tm, d), lambda i, j: (i, j, 0)),
                  pl.BlockSpec((1, d), lambda i, j: (0, 0)),
                  pl.BlockSpec((1, 1, d), mod_map),
                  pl.BlockSpec((1, 1, d), mod_map)],
        out_specs=pl.BlockSpec((1, tm, d), lambda i, j: (i, j, 0)),
        compiler_params=_params("arbitrary", "arbitrary"),
        name="norm_modulate",
    )(x, w.reshape(1, d), shift, scale)


def _rmsnorm_kernel(x_ref, w_ref, o_ref):
    x = x_ref[...]
    ms = jnp.mean(x * x, axis=-1, keepdims=True)
    o_ref[...] = x * lax.rsqrt(ms + EPS) * w_ref[...]


def rms_norm_rows(x, w):
    m, d = x.shape
    tm = _tile(m, (512, 256, 128))
    return pl.pallas_call(
        _rmsnorm_kernel,
        out_shape=jax.ShapeDtypeStruct((m, d), F32),
        grid=(m // tm,),
        in_specs=[pl.BlockSpec((tm, d), lambda i: (i, 0)),
                  pl.BlockSpec((1, d), lambda i: (0, 0))],
        out_specs=pl.BlockSpec((tm, d), lambda i: (i, 0)),
        compiler_params=_params("arbitrary"),
        name="final_rms_norm",
    )(x, w.reshape(1, d))


def _mm_kernel(a_ref, w_ref, o_ref, *, act):
    acc = _dot(a_ref[...], w_ref[...])
    if act == "sigmoid":
        acc = _sigmoid(acc)
    o_ref[...] = acc.astype(o_ref.dtype)


def matmul(a, w, out_dtype, act=None):
    m, k = a.shape
    n = w.shape[1]
    tm = _tile(m, (1024, 512, 256, 128))
    tn = _tile(n, (1024, 512, 256, 128))
    return pl.pallas_call(
        functools.partial(_mm_kernel, act=act),
        out_shape=jax.ShapeDtypeStruct((m, n), out_dtype),
        grid=(m // tm, n // tn),
        in_specs=[pl.BlockSpec((tm, k), lambda i, j: (i, 0)),
                  pl.BlockSpec((k, tn), lambda i, j: (0, j))],
        out_specs=pl.BlockSpec((tm, tn), lambda i, j: (i, j)),
        compiler_params=_params("arbitrary", "arbitrary"),
        name="matmul_" + (act or "plain"),
    )(a, w)


def _mm_res_kernel(a_ref, w_ref, r_ref, m_ref, o_ref):
    o_ref[...] = r_ref[...] + m_ref[0] * _dot(a_ref[...], w_ref[...])


def matmul_residual(a, w, res, mod):
    m, k = a.shape
    n = w.shape[1]
    rows_per_mod = m // mod.shape[0]
    tm = _tile(rows_per_mod, (1024, 512, 256, 128))
    tn = _tile(n, (1024, 512, 256, 128))
    return pl.pallas_call(
        _mm_res_kernel,
        out_shape=jax.ShapeDtypeStruct((m, n), F32),
        grid=(m // tm, n // tn),
        in_specs=[pl.BlockSpec((tm, k), lambda i, j: (i, 0)),
                  pl.BlockSpec((k, tn), lambda i, j: (0, j)),
                  pl.BlockSpec((tm, tn), lambda i, j: (i, j)),
                  pl.BlockSpec((1, 1, tn), lambda i, j: (i * tm // rows_per_mod, 0, j))],
        out_specs=pl.BlockSpec((tm, tn), lambda i, j: (i, j)),
        compiler_params=_params("arbitrary", "arbitrary"),
        name="matmul_residual",
    )(a, w, res, mod)


def _merge_kernel(ya_ref, yb_ref, yc_ref, g0_ref, g1_ref, g2_ref, wb_ref, o_ref):
    acc = g0_ref[...].astype(F32) * _dot(ya_ref[...], wb_ref[0])
    acc = acc + g1_ref[...].astype(F32) * _dot(yb_ref[...], wb_ref[1])
    acc = acc + g2_ref[...].astype(F32) * _dot(yc_ref[...], wb_ref[2])
    o_ref[...] = acc.astype(o_ref.dtype)


def merge_branches(ya, yb, yc, gates, wb):
    m, kb = ya.shape
    d = wb.shape[2]
    tm = _tile(m, (1024, 512, 256, 128))
    tn = _tile(d, (512, 256, 128))
    nj = d // tn
    y_spec = pl.BlockSpec((tm, kb), lambda i, j: (i, 0))

    def g_spec(nb):
        return pl.BlockSpec((tm, tn), lambda i, j: (i, nb * nj + j))

    return pl.pallas_call(
        _merge_kernel,
        out_shape=jax.ShapeDtypeStruct((m, d), BF16),
        grid=(m // tm, nj),
        in_specs=[y_spec, y_spec, y_spec, g_spec(0), g_spec(1), g_spec(2),
                  pl.BlockSpec((N_BRANCHES, kb, tn), lambda i, j: (0, 0, j))],
        out_specs=pl.BlockSpec((tm, tn), lambda i, j: (i, j)),
        compiler_params=_params("arbitrary", "arbitrary"),
        name="merge_branches",
    )(ya, yb, yc, gates, gates, gates, wb)


def _ffn_kernel(h_ref, wg_ref, wu_ref, wd_ref, x_ref, m_ref, o_ref, acc_ref):
    f = pl.program_id(1)

    @pl.when(f == 0)
    def _():
        acc_ref[...] = jnp.zeros_like(acc_ref)

    h = h_ref[...]
    a = _silu(_dot(h, wg_ref[...])) * _dot(h, wu_ref[...])
    acc_ref[...] += _dot(a.astype(BF16), wd_ref[...])

    @pl.when(f == pl.num_programs(1) - 1)
    def _():
        o_ref[...] = x_ref[...] + m_ref[0] * acc_ref[...]


def swiglu_residual(h, wg, wu, wd, x, mod):
    m, d = h.shape
    ff = wg.shape[1]
    rows_per_mod = m // mod.shape[0]
    tm = _tile(rows_per_mod, (512, 256, 128))
    tf = _tile(ff, (512, 256, 128))
    return pl.pallas_call(
        _ffn_kernel,
        out_shape=jax.ShapeDtypeStruct((m, d), F32),
        grid=(m // tm, ff // tf),
        in_specs=[pl.BlockSpec((tm, d), lambda i, f: (i, 0)),
                  pl.BlockSpec((d, tf), lambda i, f: (0, f)),
                  pl.BlockSpec((d, tf), lambda i, f: (0, f)),
                  pl.BlockSpec((tf, d), lambda i, f: (f, 0)),
                  pl.BlockSpec((tm, d), lambda i, f: (i, 0)),
                  pl.BlockSpec((1, 1, d), lambda i, f: (i * tm // rows_per_mod, 0, 0))],
        out_specs=pl.BlockSpec((tm, d), lambda i, f: (i, 0)),
        scratch_shapes=[pltpu.VMEM((tm, d), F32)],
        compiler_params=_params("arbitrary", "arbitrary"),
        name="swiglu_residual",
    )(h, wg, wu, wd, x, mod)


MOE_TILE = 512
ROUTE_E0, ROUTE_E1, ROUTE_W0, ROUTE_W1, ROUTE_P0, ROUTE_P1 = range(6)


def _router_kernel(x_ref, w_ref, sh_ref, sc_ref, rw_ref, h_ref, r_ref, cnt_ref, run_ref):
    @pl.when((pl.program_id(0) == 0) & (pl.program_id(1) == 0))
    def _():
        run_ref[...] = jnp.zeros_like(run_ref)

    x = x_ref[0]
    ms = jnp.mean(x * x, axis=-1, keepdims=True)
    h = x * lax.rsqrt(ms + EPS) * w_ref[...] * (1.0 + sc_ref[0]) + sh_ref[0]
    h_ref[0] = h
    logits = jnp.dot(h, rw_ref[...], preferred_element_type=F32, precision=lax.Precision.HIGHEST)
    lane = lax.broadcasted_iota(jnp.int32, logits.shape, 1)
    logits = jnp.where(lane < N_EXPERTS, logits, -jnp.inf)
    m1 = jnp.max(logits, axis=-1, keepdims=True)
    i1 = jnp.min(jnp.where(logits == m1, lane, LANES), axis=-1, keepdims=True)
    rest = jnp.where(lane == i1, -jnp.inf, logits)
    m2 = jnp.max(rest, axis=-1, keepdims=True)
    i2 = jnp.min(jnp.where(rest == m2, lane, LANES), axis=-1, keepdims=True)
    e2 = jnp.exp(m2 - m1)
    den = 1.0 + e2
    sel1 = lane == i1
    sel2 = lane == i2
    onehot = jnp.where(sel1 | sel2, 1.0, 0.0)
    tm = onehot.shape[0]
    tri = jnp.where(lax.broadcasted_iota(jnp.int32, (tm, tm), 0) > lax.broadcasted_iota(jnp.int32, (tm, tm), 1),
                    1.0, 0.0).astype(BF16)
    ranks = _dot(tri, onehot.astype(BF16)) + run_ref[...]
    p1 = jnp.sum(jnp.where(sel1, ranks, 0.0), axis=-1, keepdims=True)
    p2 = jnp.sum(jnp.where(sel2, ranks, 0.0), axis=-1, keepdims=True)
    run_ref[...] += jnp.sum(onehot, axis=0, keepdims=True)
    cnt_ref[...] = jnp.broadcast_to(run_ref[...], cnt_ref.shape)
    rec = jnp.where(lane == ROUTE_E0, i1.astype(F32), 0.0)
    rec = jnp.where(lane == ROUTE_E1, i2.astype(F32), rec)
    rec = jnp.where(lane == ROUTE_W0, 1.0 / den, rec)
    rec = jnp.where(lane == ROUTE_W1, e2 / den, rec)
    rec = jnp.where(lane == ROUTE_P0, p1, rec)
    rec = jnp.where(lane == ROUTE_P1, p2, rec)
    r_ref[0] = rec


def route_top2(x, w, shift, scale, router_w):
    b, t, d = x.shape
    tm = _tile(t, (512, 256, 128))
    rw = jnp.zeros((d, LANES), F32).at[:, :N_EXPERTS].set(router_w)
    return pl.pallas_call(
        _router_kernel,
        out_shape=(jax.ShapeDtypeStruct((b, t, d), F32), jax.ShapeDtypeStruct((b, t, LANES), F32),
                   jax.ShapeDtypeStruct((SUBLANES, LANES), F32)),
        grid=(b, t // tm),
        in_specs=[pl.BlockSpec((1, tm, d), lambda i, j: (i, j, 0)),
                  pl.BlockSpec((1, d), lambda i, j: (0, 0)),
                  pl.BlockSpec((1, 1, d), lambda i, j: (i, 0, 0)),
                  pl.BlockSpec((1, 1, d), lambda i, j: (i, 0, 0)),
                  pl.BlockSpec((d, LANES), lambda i, j: (0, 0))],
        out_specs=(pl.BlockSpec((1, tm, d), lambda i, j: (i, j, 0)),
                   pl.BlockSpec((1, tm, LANES), lambda i, j: (i, j, 0)),
                   pl.BlockSpec((SUBLANES, LANES), lambda i, j: (0, 0))),
        scratch_shapes=[pltpu.VMEM((1, LANES), F32)],
        compiler_params=_params("arbitrary", "arbitrary"),
        name="moe_router",
    )(x, w.reshape(1, d), shift, scale, rw)


def _row_copy(src, src_row, dst, dst_row, sem):
    return pltpu.make_async_copy(src.at[pl.ds(src_row, 1)], dst.at[pl.ds(dst_row, 1)], sem)


def _dispatch_kernel(d0_ref, d1_ref, h_hbm, init_hbm, hs_hbm, sem, *, chunk):
    del init_hbm
    base = pl.program_id(0) * chunk

    def issue(r, carry):
        t = base + r
        _row_copy(h_hbm, t, hs_hbm, d0_ref[t], sem).start()
        _row_copy(h_hbm, t, hs_hbm, d1_ref[t], sem).start()
        return carry

    def drain(r, carry):
        _row_copy(h_hbm, 0, hs_hbm, 0, sem).wait()
        _row_copy(h_hbm, 0, hs_hbm, 0, sem).wait()
        return carry

    lax.fori_loop(0, chunk, issue, 0)
    lax.fori_loop(0, chunk, drain, 0)


def moe_dispatch(h, dest0, dest1, n_rows):
    m, d = h.shape
    chunk = _tile(m, (1024, 512, 256, 128))
    any_spec = pl.BlockSpec(memory_space=pl.ANY)
    return pl.pallas_call(
        functools.partial(_dispatch_kernel, chunk=chunk),
        out_shape=jax.ShapeDtypeStruct((n_rows, d), F32),
        grid_spec=pltpu.PrefetchScalarGridSpec(
            num_scalar_prefetch=2, grid=(m // chunk,),
            in_specs=[any_spec, any_spec], out_specs=any_spec,
            scratch_shapes=[pltpu.SemaphoreType.DMA]),
        input_output_aliases={3: 0},
        compiler_params=_params("arbitrary"),
        name="moe_dispatch",
    )(dest0, dest1, h, jnp.zeros((n_rows, d), F32))


def _grouped_ffn_kernel(te_ref, nu_ref, hs_ref, wg_ref, wu_ref, wd_ref, o_ref, acc_ref, hb_ref):
    del te_ref
    i = pl.program_id(0)
    f = pl.program_id(1)

    @pl.when(f == 0)
    def _():
        acc_ref[...] = jnp.zeros_like(acc_ref)
        hb_ref[...] = hs_ref[...].astype(BF16)

    @pl.when(i < nu_ref[0])
    def _():
        h = hb_ref[...]
        a = _silu(_dot(h, wg_ref[0])) * _dot(h, wu_ref[0])
        acc_ref[...] += _dot(a.astype(BF16), wd_ref[0])

    @pl.when(f == pl.num_programs(1) - 1)
    def _():
        o_ref[...] = acc_ref[...]


def moe_grouped_swiglu(hs, tile_expert, n_used, wg, wu, wd):
    p, d = hs.shape
    ff = wg.shape[2]
    tm = MOE_TILE
    tf = _tile(ff, (512, 256, 128))

    def fblock(i, f, nu):
        return jnp.where(i < nu[0], f, 0)

    return pl.pallas_call(
        _grouped_ffn_kernel,
        out_shape=jax.ShapeDtypeStruct((p, d), F32),
        grid_spec=pltpu.PrefetchScalarGridSpec(
            num_scalar_prefetch=2, grid=(p // tm, ff // tf),
            in_specs=[pl.BlockSpec((tm, d), lambda i, f, te, nu: (i, 0)),
                      pl.BlockSpec((1, d, tf), lambda i, f, te, nu: (te[i], 0, fblock(i, f, nu))),
                      pl.BlockSpec((1, d, tf), lambda i, f, te, nu: (te[i], 0, fblock(i, f, nu))),
                      pl.BlockSpec((1, tf, d), lambda i, f, te, nu: (te[i], fblock(i, f, nu), 0))],
            out_specs=pl.BlockSpec((tm, d), lambda i, f, te, nu: (i, 0)),
            scratch_shapes=[pltpu.VMEM((tm, d), F32), pltpu.VMEM((tm, d), BF16)]),
        compiler_params=_params("arbitrary", "arbitrary"),
        name="moe_grouped_swiglu",
    )(tile_expert, n_used, hs, wg, wu, wd)


def _combine_kernel(d0_ref, d1_ref, ys_hbm, r_ref, x_ref, m_ref, o_ref, buf_ref, sem, *, tm):
    base = pl.program_id(0) * tm

    def issue(r, carry):
        t = base + r
        _row_copy(ys_hbm, d0_ref[t], buf_ref.at[0], r, sem).start()
        _row_copy(ys_hbm, d1_ref[t], buf_ref.at[1], r, sem).start()
        return carry

    def drain(r, carry):
        _row_copy(ys_hbm, 0, buf_ref.at[0], 0, sem).wait()
        _row_copy(ys_hbm, 0, buf_ref.at[1], 0, sem).wait()
        return carry

    lax.fori_loop(0, tm, issue, 0)
    lax.fori_loop(0, tm, drain, 0)
    rec = r_ref[...]
    lane = lax.broadcasted_iota(jnp.int32, rec.shape, 1)
    w0 = jnp.sum(jnp.where(lane == ROUTE_W0, rec, 0.0), axis=-1, keepdims=True)
    w1 = jnp.sum(jnp.where(lane == ROUTE_W1, rec, 0.0), axis=-1, keepdims=True)
    o_ref[...] = x_ref[...] + m_ref[0] * (w0 * buf_ref[0] + w1 * buf_ref[1])


def moe_combine(ys, dest0, dest1, route, x, mod):
    m, d = x.shape
    rows_per_mod = m // mod.shape[0]
    tm = _tile(rows_per_mod, (256, 128))
    return pl.pallas_call(
        functools.partial(_combine_kernel, tm=tm),
        out_shape=jax.ShapeDtypeStruct((m, d), F32),
        grid_spec=pltpu.PrefetchScalarGridSpec(
            num_scalar_prefetch=2, grid=(m // tm,),
            in_specs=[pl.BlockSpec(memory_space=pl.ANY),
                      pl.BlockSpec((tm, LANES), lambda i, d0, d1: (i, 0)),
                      pl.BlockSpec((tm, d), lambda i, d0, d1: (i, 0)),
                      pl.BlockSpec((1, 1, d), lambda i, d0, d1: (i * tm // rows_per_mod, 0, 0))],
            out_specs=pl.BlockSpec((tm, d), lambda i, d0, d1: (i, 0)),
            scratch_shapes=[pltpu.VMEM((2, tm, d), F32), pltpu.SemaphoreType.DMA]),
        compiler_params=_params("arbitrary"),
        name="moe_combine",
    )(dest0, dest1, ys, route, x, mod)


def moe_swiglu_residual(x3, norm_w, shift, scale, router_w, wg, wu, wd, mod):
    b, t, d = x3.shape
    m = b * t
    h, route, counts = route_top2(x3, norm_w, shift, scale, router_w)
    route = route.reshape(m, LANES)
    cnt = counts[0, :N_EXPERTS].astype(jnp.int32)
    padded = (cnt + MOE_TILE - 1) // MOE_TILE * MOE_TILE
    ends = jnp.cumsum(padded)
    starts = ends - padded
    e0 = route[:, ROUTE_E0].astype(jnp.int32)
    e1 = route[:, ROUTE_E1].astype(jnp.int32)
    dest0 = starts[e0] + route[:, ROUTE_P0].astype(jnp.int32)
    dest1 = starts[e1] + route[:, ROUTE_P1].astype(jnp.int32)
    n_rows = 2 * m + N_EXPERTS * MOE_TILE
    n_tiles = n_rows // MOE_TILE
    n_used = ends[-1] // MOE_TILE
    tile_ids = jnp.minimum(jnp.arange(n_tiles, dtype=jnp.int32), n_used - 1)
    tile_expert = jnp.sum((tile_ids[:, None] >= (ends // MOE_TILE)[None, :]).astype(jnp.int32), axis=1)
    tile_expert = jnp.minimum(tile_expert, N_EXPERTS - 1).astype(jnp.int32)
    hs = moe_dispatch(h.reshape(m, d), dest0, dest1, n_rows)
    ys = moe_grouped_swiglu(hs, tile_expert, n_used.reshape(1).astype(jnp.int32), wg, wu, wd)
    return moe_combine(ys, dest0, dest1, route, x3.reshape(m, d), mod)


def _hgrn2_kernel(q_ref, f_ref, v_ref, lb_ref, s0_ref, o_ref, sout_ref, st_ref, c_ref, k_ref,
                  *, reverse, rows):
    j = pl.program_id(1)

    @pl.when(j == 0)
    def _():
        st_ref[...] = s0_ref[0]

    lb = lb_ref[...]
    fr = f_ref[0]
    log_sig = jnp.minimum(fr, 0.0) - jnp.log(1.0 + jnp.exp(-jnp.abs(fr)))
    a = jnp.log(lb)
    b = jnp.log(1.0 - lb) + log_sig
    log_f = jnp.maximum(a, b) + jnp.log(1.0 + jnp.exp(-jnp.abs(a - b)))
    k_ref[...] = (1.0 - lb) / (1.0 + jnp.exp(fr))
    rib = lax.broadcasted_iota(jnp.int32, fr.shape, 0) & (HG_BLOCK - 1)
    c = log_f
    sh = 1
    while sh < HG_BLOCK:
        if reverse:
            c = c + jnp.where(rib < HG_BLOCK - sh, pltpu.roll(c, rows - sh, axis=0), 0.0)
        else:
            c = c + jnp.where(rib >= sh, pltpu.roll(c, sh, axis=0), 0.0)
        sh *= 2
    c_ref[...] = c

    trow = lax.broadcasted_iota(jnp.int32, (HG_BLOCK, HG_DK), 0)
    blocks = list(range(rows // HG_BLOCK))
    if reverse:
        blocks = blocks[::-1]

    def one_head(h):
        off = pl.multiple_of(h * HG_DK, HG_DK)
        s_t = st_ref[h]
        for blk in blocks:
            r0 = blk * HG_BLOCK
            qb = q_ref[0, r0:r0 + HG_BLOCK, pl.ds(off, HG_DK)] * (HG_DK ** -0.5)
            vb = v_ref[0, r0:r0 + HG_BLOCK, pl.ds(off, HG_DK)]
            kb = k_ref[r0:r0 + HG_BLOCK, pl.ds(off, HG_DK)]
            cb = c_ref[r0:r0 + HG_BLOCK, pl.ds(off, HG_DK)]
            o = _dot_nt((qb * jnp.exp(cb)).astype(BF16), s_t.astype(BF16))
            for jj in range(HG_BLOCK):
                valid = (trow <= jj) if reverse else (trow >= jj)
                e = jnp.exp(jnp.where(valid, cb - cb[jj:jj + 1], -jnp.inf)) * (qb * kb[jj:jj + 1])
                o = o + jnp.sum(e, axis=-1, keepdims=True) * vb[jj:jj + 1]
            c_tot = cb[0:1] if reverse else cb[HG_BLOCK - 1:HG_BLOCK]
            kt = (kb * jnp.exp(c_tot - cb)).astype(BF16)
            s_t = s_t * jnp.exp(c_tot) + _dot_tn(vb.astype(BF16), kt)
            o_ref[0, r0:r0 + HG_BLOCK, pl.ds(off, HG_DK)] = o
        st_ref[h] = s_t

    def head_pair(h, carry):
        one_head(h)
        one_head(h + HG_HEADS // 2)
        return carry

    lax.fori_loop(0, HG_HEADS // 2, head_pair, 0)

    @pl.when(j == pl.num_programs(1) - 1)
    def _():
        sout_ref[0] = st_ref[...]


def hgrn2_scan(p_hg, lb, s0, reverse):
    b, t, _ = p_hg.shape
    w = HG_WIDTH
    rows = _tile(t, (64, 32, 16))
    nt = t // rows
    tmap = (lambda j: nt - 1 - j) if reverse else (lambda j: j)
    fcol = 2 if reverse else 1
    return pl.pallas_call(
        functools.partial(_hgrn2_kernel, reverse=reverse, rows=rows),
        out_shape=(jax.ShapeDtypeStruct((b, t, w), F32),
                   jax.ShapeDtypeStruct(s0.shape, F32)),
        grid=(b, nt),
        in_specs=[pl.BlockSpec((1, rows, w), lambda i, j: (i, tmap(j), 0)),
                  pl.BlockSpec((1, rows, w), lambda i, j: (i, tmap(j), fcol)),
                  pl.BlockSpec((1, rows, w), lambda i, j: (i, tmap(j), 3)),
                  pl.BlockSpec((1, w), lambda i, j: (0, 0)),
                  pl.BlockSpec((1, HG_HEADS, HG_DK, HG_DK), lambda i, j: (i, 0, 0, 0))],
        out_specs=(pl.BlockSpec((1, rows, w), lambda i, j: (i, tmap(j), 0)),
                   pl.BlockSpec((1, HG_HEADS, HG_DK, HG_DK), lambda i, j: (i, 0, 0, 0))),
        scratch_shapes=[pltpu.VMEM((HG_HEADS, HG_DK, HG_DK), F32),
                        pltpu.VMEM((rows, w), F32),
                        pltpu.VMEM((rows, w), F32)],
        compiler_params=_params("arbitrary", "arbitrary"),
        name="hgrn2_scan_bwd" if reverse else "hgrn2_scan_fwd",
    )(p_hg, p_hg, p_hg, lb.reshape(1, w), s0)


def _hg_readout_kernel(of_ref, ob_ref, g_ref, w_ref, y_ref):
    for h in range(HG_HEADS):
        sl = slice(h * HG_DK, (h + 1) * HG_DK)
        o = of_ref[:, sl] + ob_ref[:, sl]
        ms = jnp.mean(o * o, axis=-1, keepdims=True)
        y = o * lax.rsqrt(ms + EPS) * w_ref[:, sl]
        y_ref[:, sl] = (y * _silu(g_ref[:, sl])).astype(y_ref.dtype)


def hgrn2_readout(o_f, o_b, p_hg, norm_w):
    m, w = o_f.shape
    tm = _tile(m, (512, 256, 128))
    spec = pl.BlockSpec((tm, w), lambda i: (i, 0))
    return pl.pallas_call(
        _hg_readout_kernel,
        out_shape=jax.ShapeDtypeStruct((m, w), BF16),
        grid=(m // tm,),
        in_specs=[spec, spec, pl.BlockSpec((tm, w), lambda i: (i, 4)),
                  pl.BlockSpec((1, w), lambda i: (0, 0))],
        out_specs=spec,
        compiler_params=_params("arbitrary"),
        name="hgrn2_readout",
    )(o_f, o_b, p_hg, norm_w.reshape(1, w))


def _conv_kernel(prev_ref, cur_ref, next_ref, w_ref, b_ref, o_ref, *, tm):
    i = pl.program_id(1)
    prev = jnp.where(i > 0, prev_ref[0], 0.0)
    nxt = jnp.where(i < pl.num_programs(1) - 1, next_ref[0], 0.0)
    xe = jnp.concatenate([prev, cur_ref[0], nxt], axis=0)
    ext = tm + 2 * SUBLANES
    acc = jnp.zeros((tm, xe.shape[1]), F32) + b_ref[...]
    for k in range(SSM_CONV):
        sh = (SSM_CONV // 2 - k) % ext
        xs = xe if sh == 0 else pltpu.roll(xe, sh, axis=0)
        acc = acc + w_ref[k:k + 1, :] * xs[SUBLANES:SUBLANES + tm]
    o_ref[0] = _silu(acc)


def ssd_conv_silu(p_ssd, conv_w, conv_b):
    b, t, _ = p_ssd.shape
    ch = SSM_CONV_CH
    tm = _tile(t, (512, 256, 128))
    tc = 1024
    c0 = SSM_WIDTH // tc
    hb = tm // SUBLANES
    nh = t // SUBLANES
    return pl.pallas_call(
        functools.partial(_conv_kernel, tm=tm),
        out_shape=jax.ShapeDtypeStruct((b, t, ch), F32),
        grid=(b, t // tm, ch // tc),
        in_specs=[pl.BlockSpec((1, SUBLANES, tc), lambda i, j, cj: (i, jnp.maximum(j * hb - 1, 0), c0 + cj)),
                  pl.BlockSpec((1, tm, tc), lambda i, j, cj: (i, j, c0 + cj)),
                  pl.BlockSpec((1, SUBLANES, tc), lambda i, j, cj: (i, jnp.minimum((j + 1) * hb, nh - 1), c0 + cj)),
                  pl.BlockSpec((SUBLANES, tc), lambda i, j, cj: (0, cj)),
                  pl.BlockSpec((1, tc), lambda i, j, cj: (0, cj))],
        out_specs=pl.BlockSpec((1, tm, tc), lambda i, j, cj: (i, j, cj)),
        compiler_params=_params("arbitrary", "arbitrary", "arbitrary"),
        name="ssd_conv_silu",
    )(p_ssd, p_ssd, p_ssd,
      jnp.zeros((SUBLANES, ch), F32).at[:SSM_CONV].set(conv_w), conv_b.reshape(1, ch))


def _split3(x):
    hi = x.astype(BF16)
    r1 = x - hi.astype(F32)
    mid = r1.astype(BF16)
    lo = (r1 - mid.astype(F32)).astype(BF16)
    return hi, mid, lo


def _pick_lanes(parts, sel):
    return _dot(parts[0], sel) + _dot(parts[1], sel) + _dot(parts[2], sel)


def _ssd_kernel(x_ref, b_ref, c_ref, dt_ref, prm_ref, selh_ref, selp_ref, h0_ref, y_ref, hout_ref, hst_ref,
                *, reverse):
    j = pl.program_id(1)
    cl = SSM_CHUNK
    hd = SSM_HEAD_DIM

    @pl.when(j == 0)
    def _():
        hst_ref[...] = h0_ref[0]

    prm = prm_ref[...]
    dtr = dt_ref[0] + prm[0:1]
    dt = jnp.maximum(dtr, 0.0) + jnp.log(1.0 + jnp.exp(-jnp.abs(dtr)))
    a = dt * prm[1:2]
    row = lax.broadcasted_iota(jnp.int32, a.shape, 0)
    ac = a
    sh = 1
    while sh < cl:
        if reverse:
            ac = ac + jnp.where(row < cl - sh, pltpu.roll(ac, cl - sh, axis=0), 0.0)
        else:
            ac = ac + jnp.where(row >= sh, pltpu.roll(ac, sh, axis=0), 0.0)
        sh *= 2
    ac_t = ac.T
    ac_parts = _split3(ac)
    ac_head = _pick_lanes(ac_parts, selh_ref[...])
    ac_x = _pick_lanes(ac_parts, selp_ref[...])
    dt_x = _pick_lanes(_split3(dt), selp_ref[...])
    d_x = _pick_lanes(_split3(prm), selp_ref[...])[2:3]
    tot_x = ac_x[0:1] if reverse else ac_x[cl - 1:cl]
    xs = x_ref[0]
    xdt = xs * dt_x
    y_skip = xs * d_x
    exp_ac = jnp.exp(ac_x)
    xdec = (xdt * jnp.exp(tot_x - ac_x)).astype(BF16)
    exp_tot = jnp.exp(tot_x)
    xdt = xdt.astype(BF16)

    ti = lax.broadcasted_iota(jnp.int32, (cl, cl), 0)
    si = lax.broadcasted_iota(jnp.int32, (cl, cl), 1)
    mask = (ti <= si) if reverse else (ti >= si)
    first_head = si < hd
    zero = jnp.zeros((), BF16)
    lane0 = SSM_HEADS if reverse else 0
    pairs_per_group = SSM_HEADS // SSM_GROUPS // 2
    for g in range(SSM_GROUPS):
        bg = b_ref[0, :, g * SSM_STATE:(g + 1) * SSM_STATE].astype(BF16)
        cg = c_ref[0, :, g * SSM_STATE:(g + 1) * SSM_STATE].astype(BF16)
        cb = _dot_nt(cg, bg)
        for r in range(pairs_per_group):
            p = g * pairs_per_group + r
            cols = slice(p * 2 * hd, (p + 1) * 2 * hd)
            xp = xdt[:, cols]
            y = None
            for k, keep in ((0, first_head), (1, ~first_head)):
                h = 2 * p + k
                arow = ac_t[lane0 + h:lane0 + h + 1, :]
                acol = ac_head[:, h * LANES:(h + 1) * LANES]
                w = cb * jnp.exp(jnp.where(mask, acol - arow, -jnp.inf))
                yk = _dot(w.astype(BF16), jnp.where(keep, xp, zero))
                y = yk if y is None else y + yk
            hp = hst_ref[p]
            y = y + exp_ac[:, cols] * _dot_nt(cg, hp.astype(BF16)) + y_skip[:, cols]
            grow = jnp.where(ti < hd, exp_tot[:, p * 2 * hd:p * 2 * hd + 1], exp_tot[:, (p + 1) * 2 * hd - 1:(p + 1) * 2 * hd])
            hst_ref[p] = hp * grow + _dot_tn(xdec[:, cols], bg)
            y_ref[0, :, cols] = y

    @pl.when(j == pl.num_programs(1) - 1)
    def _():
        hout_ref[0] = hst_ref[...]


def ssd_scan(xbc, dt_raw, prm, h0, reverse):
    b, t, _ = xbc.shape
    cl = SSM_CHUNK
    nt = t // cl
    gw = SSM_GROUPS * SSM_STATE
    tmap = (lambda j: nt - 1 - j) if reverse else (lambda j: j)
    lane0 = SSM_HEADS if reverse else 0
    src = jnp.arange(LANES, dtype=jnp.int32)[:, None]
    sel_head = (src == lane0 + jnp.arange(SSM_HEADS * LANES, dtype=jnp.int32)[None, :] // LANES).astype(BF16)
    sel_x = (src == lane0 + jnp.arange(SSM_WIDTH, dtype=jnp.int32)[None, :] // SSM_HEAD_DIM).astype(BF16)
    st_spec = pl.BlockSpec((1,) + h0.shape[1:], lambda i, j: (i, 0, 0, 0))
    return pl.pallas_call(
        functools.partial(_ssd_kernel, reverse=reverse),
        out_shape=(jax.ShapeDtypeStruct((b, t, SSM_WIDTH), F32),
                   jax.ShapeDtypeStruct(h0.shape, F32)),
        grid=(b, nt),
        in_specs=[pl.BlockSpec((1, cl, SSM_WIDTH), lambda i, j: (i, tmap(j), 0)),
                  pl.BlockSpec((1, cl, gw), lambda i, j: (i, tmap(j), SSM_WIDTH // gw)),
                  pl.BlockSpec((1, cl, gw), lambda i, j: (i, tmap(j), SSM_WIDTH // gw + 1)),
                  pl.BlockSpec((1, cl, LANES), lambda i, j: (i, tmap(j), 0)),
                  pl.BlockSpec((SUBLANES, LANES), lambda i, j: (0, 0)),
                  pl.BlockSpec(sel_head.shape, lambda i, j: (0, 0)),
                  pl.BlockSpec(sel_x.shape, lambda i, j: (0, 0)),
                  st_spec],
        out_specs=(pl.BlockSpec((1, cl, SSM_WIDTH), lambda i, j: (i, tmap(j), 0)), st_spec),
        scratch_shapes=[pltpu.VMEM(h0.shape[1:], F32)],
        compiler_params=_params("arbitrary", "arbitrary"),
        name="ssd_scan_bwd" if reverse else "ssd_scan_fwd",
    )(xbc, xbc, xbc, dt_raw, prm, sel_head, sel_x, h0)


def _ssd_readout_kernel(yf_ref, yb_ref, z_ref, w_ref, o_ref):
    gw = SSM_WIDTH // SSM_GROUPS
    for g in range(SSM_GROUPS):
        sl = slice(g * gw, (g + 1) * gw)
        y = (yf_ref[:, sl] + yb_ref[:, sl]) * _silu(z_ref[:, sl])
        ms = jnp.mean(y * y, axis=-1, keepdims=True)
        o_ref[:, sl] = (y * lax.rsqrt(ms + EPS) * w_ref[:, sl]).astype(o_ref.dtype)


def ssd_readout(y_f, y_b, p_ssd, norm_w):
    m, w = y_f.shape
    tm = _tile(m, (512, 256, 128))
    spec = pl.BlockSpec((tm, w), lambda i: (i, 0))
    return pl.pallas_call(
        _ssd_readout_kernel,
        out_shape=jax.ShapeDtypeStruct((m, w), BF16),
        grid=(m // tm,),
        in_specs=[spec, spec, spec, pl.BlockSpec((1, w), lambda i: (0, 0))],
        out_specs=spec,
        compiler_params=_params("arbitrary"),
        name="ssd_readout",
    )(y_f, y_b, p_ssd, norm_w.reshape(1, w))


def _attprep_kernel(p_ref, qw_ref, kw_ref, cos_ref, sin_ref, q_ref, k_ref, v_ref, *, rope):
    lane = lax.broadcasted_iota(jnp.int32, (p_ref.shape[0], ATT_HEAD_DIM), 1)
    first = (lane % (ATT_HEAD_DIM // 2)) < (ATT_HEAD_DIM // 4)
    for h in range(ATT_HEADS + ATT_KV_HEADS):
        xh = p_ref[:, h * ATT_HEAD_DIM:(h + 1) * ATT_HEAD_DIM]
        w = qw_ref[...] if h < ATT_HEADS else kw_ref[...]
        ms = jnp.mean(xh * xh, axis=-1, keepdims=True)
        y = xh * lax.rsqrt(ms + EPS) * w
        if rope:
            partner = jnp.where(first, pltpu.roll(y, 3 * ATT_HEAD_DIM // 4, axis=1),
                                pltpu.roll(y, ATT_HEAD_DIM // 4, axis=1))
            y = y * cos_ref[...] + partner * sin_ref[...]
        if h < ATT_HEADS:
            q_ref[:, h * ATT_HEAD_DIM:(h + 1) * ATT_HEAD_DIM] = (y * ATT_Q_SCALE).astype(q_ref.dtype)
        else:
            hk = h - ATT_HEADS
            k_ref[:, hk * ATT_HEAD_DIM:(hk + 1) * ATT_HEAD_DIM] = y.astype(k_ref.dtype)
            v0 = ATT_WIDTH + ATT_KV_WIDTH + hk * ATT_HEAD_DIM
            v_ref[:, 2 * hk * ATT_HEAD_DIM:(2 * hk + 1) * ATT_HEAD_DIM] = p_ref[:, v0:v0 + ATT_HEAD_DIM].astype(v_ref.dtype)
            v_ref[:, (2 * hk + 1) * ATT_HEAD_DIM:(2 * hk + 2) * ATT_HEAD_DIM] = jnp.ones(
                (p_ref.shape[0], ATT_HEAD_DIM), v_ref.dtype)


def attention_prep(p_att, q_norm_w, k_norm_w, cos, sin, seq, rope):
    m, wtot = p_att.shape
    tm = _tile(seq, (512, 256, 128))
    npos = seq // tm
    tab = pl.BlockSpec((tm, ATT_HEAD_DIM), (lambda i: (i % npos, 0)) if rope else (lambda i: (0, 0)))
    wspec = pl.BlockSpec((1, ATT_HEAD_DIM), lambda i: (0, 0))
    return pl.pallas_call(
        functools.partial(_attprep_kernel, rope=rope),
        out_shape=(jax.ShapeDtypeStruct((m, ATT_WIDTH), BF16),
                   jax.ShapeDtypeStruct((m, ATT_KV_WIDTH), BF16),
                   jax.ShapeDtypeStruct((m, 2 * ATT_KV_WIDTH), BF16)),
        grid=(m // tm,),
        in_specs=[pl.BlockSpec((tm, wtot), lambda i: (i, 0)), wspec, wspec, tab, tab],
        out_specs=(pl.BlockSpec((tm, ATT_WIDTH), lambda i: (i, 0)),
                   pl.BlockSpec((tm, ATT_KV_WIDTH), lambda i: (i, 0)),
                   pl.BlockSpec((tm, 2 * ATT_KV_WIDTH), lambda i: (i, 0))),
        compiler_params=_params("arbitrary"),
        name="attention_prep_rope" if rope else "attention_prep",
    )(p_att, q_norm_w.reshape(1, -1), k_norm_w.reshape(1, -1), cos, sin)


def _attn_kernel(q_ref, k_ref, v_ref, o_ref, *, tk, nk):
    tq = q_ref.shape[1]
    qs = [q_ref[0, :, g * ATT_HEAD_DIM:(g + 1) * ATT_HEAD_DIM] for g in range(ATT_GROUP)]

    def body(c, carry):
        off = pl.multiple_of(c * tk, tk)
        kc = k_ref[0, pl.ds(off, tk), :]
        vc = v_ref[0, pl.ds(off, tk), :]
        out = []
        for g in range(ATT_GROUP):
            m, acc = carry[g]
            s = _dot_nt(qs[g], kc)
            mn = jnp.maximum(m, jnp.max(s, axis=-1, keepdims=True))
            p = jnp.exp2(s - mn)
            out.append((mn, jnp.exp2(m - mn) * acc + _dot(p.astype(BF16), vc)))
        return tuple(out)

    init = tuple((jnp.full((tq, 1), -jnp.inf, F32), jnp.zeros((tq, 2 * ATT_HEAD_DIM), F32))
                 for _ in range(ATT_GROUP))
    res = lax.fori_loop(0, nk, body, init)
    for g in range(ATT_GROUP):
        acc = res[g][1]
        o_ref[0, :, g * ATT_HEAD_DIM:(g + 1) * ATT_HEAD_DIM] = (
            acc[:, :ATT_HEAD_DIM] / acc[:, ATT_HEAD_DIM:]).astype(o_ref.dtype)


def gqa_attention(q, k, v):
    b, t, _ = q.shape
    s = k.shape[1]
    tq = _tile(t, (256, 128))
    tk = s if s <= 1024 else _tile(s, (2816, 1408, 768, 512, 384, 256, 128))
    gwid = ATT_GROUP * ATT_HEAD_DIM
    return pl.pallas_call(
        functools.partial(_attn_kernel, tk=tk, nk=s // tk),
        out_shape=jax.ShapeDtypeStruct(q.shape, BF16),
        grid=(b, ATT_KV_HEADS, t // tq),
        in_specs=[pl.BlockSpec((1, tq, gwid), lambda i, h, j: (i, j, h)),
                  pl.BlockSpec((1, s, ATT_HEAD_DIM), lambda i, h, j: (i, 0, h)),
                  pl.BlockSpec((1, s, 2 * ATT_HEAD_DIM), lambda i, h, j: (i, 0, h))],
        out_specs=pl.BlockSpec((1, tq, gwid), lambda i, h, j: (i, j, h)),
        compiler_params=_params("arbitrary", "arbitrary", "arbitrary"),
        name="gqa_attention",
    )(q, k, v)


def _rope_tables(n_tokens):
    rows = n_tokens // GRID_W
    row = jnp.repeat(jnp.arange(rows, dtype=F32), GRID_W)
    col = jnp.tile(jnp.arange(GRID_W, dtype=F32), rows)
    axis_dim = ATT_HEAD_DIM // 2
    inv = ROPE_THETA ** (-jnp.arange(0, axis_dim, 2, dtype=F32) / axis_dim)
    ar, ac = row[:, None] * inv, col[:, None] * inv
    cos = jnp.concatenate([jnp.cos(ar), jnp.cos(ar), jnp.cos(ac), jnp.cos(ac)], axis=-1)
    sin = jnp.concatenate([-jnp.sin(ar), jnp.sin(ar), -jnp.sin(ac), jnp.sin(ac)], axis=-1)
    return cos, sin


def kernel(x, c, ctx, c_ctx, ada_w, ada_b, norm_mix_w, norm_ffn_w, w_in, hg_lb_logits, hg_norm_w, ssm_conv_w, ssm_conv_b, ssm_dt_bias, ssm_a_log, ssm_d, ssm_norm_w, attn_q_norm_w, attn_k_norm_w, w_branch, w_out, ffn_w_gate, ffn_w_up, ffn_w_down, moe_router, moe_w_gate, moe_w_up, moe_w_down, final_norm_w):
    bsz, seq, d = x.shape
    clen = ctx.shape[1]
    depth = ada_w.shape[0]
    m_lat, m_ctx = bsz * seq, bsz * clen

    lb_all = jnp.cumsum(jax.nn.softmax(hg_lb_logits.astype(F32), axis=0), axis=0)
    lb_all = lb_all - lb_all[0]
    cond = jnp.zeros((SUBLANES, d), F32).at[:bsz].set(c).at[bsz].set(c_ctx)
    cos, sin = _rope_tables(seq)

    e_hg = 5 * HG_WIDTH
    e_ssd = e_hg + SSM_WIDTH + SSM_CONV_CH
    e_dt = e_ssd + 2 * SSM_HEADS
    e_att = e_dt + ATT_WIDTH + 2 * ATT_KV_WIDTH

    zeros_hg = jnp.zeros((bsz, HG_HEADS, HG_DK, HG_DK), F32)
    zeros_ssd = jnp.zeros((bsz, SSM_HEADS // 2, 2 * SSM_HEAD_DIM, SSM_STATE), F32)

    x2 = x.reshape(m_lat, d)
    xc2 = ctx.reshape(m_ctx, d)
    for l in range(depth):
        need_ctx = l < depth - 1
        mod = ada_modulation(cond, ada_w[l], ada_b[l])
        mod_lat = mod[:bsz].reshape(bsz, 6, 1, d)
        mod_ctx = mod[bsz:bsz + 1].reshape(1, 6, 1, d)
        ml = [mod_lat[:, n] for n in range(6)]
        mc = [mod_ctx[:, n] for n in range(6)]

        wl = w_in[l]
        w_hg = wl[:, :e_hg].astype(BF16)
        w_ssd = wl[:, e_hg:e_ssd].astype(BF16)
        w_dt = jnp.zeros((d, LANES), BF16).at[:, :2 * SSM_HEADS].set(wl[:, e_ssd:e_dt].astype(BF16))
        w_att = wl[:, e_dt:e_att].astype(BF16)
        w_gate = wl[:, e_att:].astype(BF16)
        w_br = w_branch[l].astype(BF16)
        w_o = w_out[l].astype(BF16)

        prm = jnp.zeros((SUBLANES, LANES), F32)
        prm = prm.at[0, :2 * SSM_HEADS].set(ssm_dt_bias[l].reshape(-1).astype(F32))
        prm = prm.at[1, :2 * SSM_HEADS].set(-jnp.exp(ssm_a_log[l].reshape(-1).astype(F32)))
        prm = prm.at[2, :2 * SSM_HEADS].set(ssm_d[l].reshape(-1).astype(F32))

        def mixer_inputs(xs, nb, t, shift, scale):
            h = norm_modulate(xs.reshape(nb, t, d), norm_mix_w[l], shift, scale).reshape(nb * t, d)
            p_hg = matmul(h, w_hg, F32).reshape(nb, t, e_hg)
            p_ssd = matmul(h, w_ssd, F32).reshape(nb, t, e_ssd - e_hg)
            p_dt = matmul(h, w_dt, F32).reshape(nb, t, LANES)
            p_att = matmul(h, w_att, F32)
            xbc = ssd_conv_silu(p_ssd, ssm_conv_w[l], ssm_conv_b[l])
            return h, p_hg, p_ssd, p_dt, p_att, xbc

        h_c, phg_c, pssd_c, pdt_c, patt_c, xbc_c = mixer_inputs(xc2, bsz, clen, mc[0], mc[1])
        h_l, phg_l, pssd_l, pdt_l, patt_l, xbc_l = mixer_inputs(x2, bsz, seq, ml[0], ml[1])

        ofc, s_f = hgrn2_scan(phg_c, lb_all[l, 0], zeros_hg, False)
        obc, s_b = hgrn2_scan(phg_c, lb_all[l, 1], zeros_hg, True)
        ofl, _ = hgrn2_scan(phg_l, lb_all[l, 0], s_f, False)
        obl, _ = hgrn2_scan(phg_l, lb_all[l, 1], s_b, True)
        ya_l = hgrn2_readout(ofl.reshape(m_lat, -1), obl.reshape(m_lat, -1), phg_l.reshape(m_lat, -1), hg_norm_w[l])

        yfc, h_f = ssd_scan(xbc_c, pdt_c, prm, zeros_ssd, False)
        ybc, h_b = ssd_scan(xbc_c, pdt_c, prm, zeros_ssd, True)
        yfl, _ = ssd_scan(xbc_l, pdt_l, prm, h_f, False)
        ybl, _ = ssd_scan(xbc_l, pdt_l, prm, h_b, True)
        yb_l = ssd_readout(yfl.reshape(m_lat, -1), ybl.reshape(m_lat, -1), pssd_l.reshape(m_lat, -1), ssm_norm_w[l])

        q_c, k_c, v_c = attention_prep(patt_c, attn_q_norm_w[l], attn_k_norm_w[l], cos, sin, clen, False)
        q_l, k_l, v_l = attention_prep(patt_l, attn_q_norm_w[l], attn_k_norm_w[l], cos, sin, seq, True)
        k_c3, v_c3 = k_c.reshape(bsz, clen, -1), v_c.reshape(bsz, clen, -1)
        k_all = jnp.concatenate([k_c3, k_l.reshape(bsz, seq, -1)], axis=1)
        v_all = jnp.concatenate([v_c3, v_l.reshape(bsz, seq, -1)], axis=1)
        yc_l = gqa_attention(q_l.reshape(bsz, seq, -1), k_all, v_all).reshape(m_lat, -1)

        gates_l = matmul(h_l, w_gate, BF16, act="sigmoid")
        merged_l = merge_branches(ya_l, yb_l, yc_l, gates_l, w_br)
        x2 = matmul_residual(merged_l, w_o, x2, ml[2])

        if need_ctx:
            ya_c = hgrn2_readout(ofc.reshape(m_ctx, -1), obc.reshape(m_ctx, -1), phg_c.reshape(m_ctx, -1), hg_norm_w[l])
            yb_c = ssd_readout(yfc.reshape(m_ctx, -1), ybc.reshape(m_ctx, -1), pssd_c.reshape(m_ctx, -1), ssm_norm_w[l])
            yc_c = gqa_attention(q_c.reshape(bsz, clen, -1), k_c3, v_c3).reshape(m_ctx, -1)
            gates_c = matmul(h_c, w_gate, BF16, act="sigmoid")
            merged_c = merge_branches(ya_c, yb_c, yc_c, gates_c, w_br)
            xc2 = matmul_residual(merged_c, w_o, xc2, mc[2])

        if l % 2 == 0:
            wg = ffn_w_gate[l // 2].astype(BF16)
            wu = ffn_w_up[l // 2].astype(BF16)
            wd = ffn_w_down[l // 2].astype(BF16)
            hf = norm_modulate(x2.reshape(bsz, seq, d), norm_ffn_w[l], ml[3], ml[4]).reshape(m_lat, d)
            x2 = swiglu_residual(hf, wg, wu, wd, x2, ml[5])
            if need_ctx:
                hfc = norm_modulate(xc2.reshape(bsz, clen, d), norm_ffn_w[l], mc[3], mc[4]).reshape(m_ctx, d)
                xc2 = swiglu_residual(hfc, wg, wu, wd, xc2, mc[5])
        else:
            wg = moe_w_gate[l // 2].astype(BF16)
            wu = moe_w_up[l // 2].astype(BF16)
            wd = moe_w_down[l // 2].astype(BF16)
            x2 = moe_swiglu_residual(x2.reshape(bsz, seq, d), norm_ffn_w[l], ml[3], ml[4], moe_router[l // 2],
                                     wg, wu, wd, ml[5])
            if need_ctx:
                xc2 = moe_swiglu_residual(xc2.reshape(bsz, clen, d), norm_ffn_w[l],
                                          jnp.broadcast_to(mc[3], (bsz, 1, d)), jnp.broadcast_to(mc[4], (bsz, 1, d)),
                                          moe_router[l // 2], wg, wu, wd, mc[5])

    return rms_norm_rows(x2, final_norm_w).reshape(bsz, seq, d)
```

```python
import functools
import math

import jax
import jax.numpy as jnp
from jax import lax
from jax.experimental import pallas as pl
from jax.experimental.pallas import tpu as pltpu

F32 = jnp.float32
BF16 = jnp.bfloat16

EPS = 1e-6
GRID_W = 64
HG_HEADS = 8
HG_DK = 128
HG_WIDTH = HG_HEADS * HG_DK
HG_BLOCK = 16
SSM_HEADS = 16
SSM_HEAD_DIM = 64
SSM_WIDTH = SSM_HEADS * SSM_HEAD_DIM
SSM_GROUPS = 4
SSM_STATE = 128
SSM_CONV = 5
SSM_CONV_CH = SSM_WIDTH + 2 * SSM_GROUPS * SSM_STATE
SSM_CHUNK = 128
ATT_HEADS = 8
ATT_KV_HEADS = 2
ATT_HEAD_DIM = 128
ATT_GROUP = ATT_HEADS // ATT_KV_HEADS
ATT_WIDTH = ATT_HEADS * ATT_HEAD_DIM
ATT_KV_WIDTH = ATT_KV_HEADS * ATT_HEAD_DIM
ATT_Q_SCALE = ATT_HEAD_DIM ** -0.5 * math.log2(math.e)
ROPE_THETA = 10000.0
N_BRANCHES = 3
N_EXPERTS = 8
LANES = 128
SUBLANES = 8
VMEM_LIMIT = 56 * 1024 * 1024


def _params(*sem):
    return pltpu.CompilerParams(dimension_semantics=sem, vmem_limit_bytes=VMEM_LIMIT)


def _tile(n, prefs):
    for t in prefs:
        if n % t == 0:
            return t
    return n


def _sigmoid(x):
    return 1.0 / (1.0 + jnp.exp(-x))


def _silu(x):
    return x / (1.0 + jnp.exp(-x))


def _dot(a, b):
    return jnp.dot(a, b, preferred_element_type=F32)


def _dot_nt(a, b):
    return lax.dot_general(a, b, (((1,), (1,)), ((), ())), preferred_element_type=F32)


def _dot_tn(a, b):
    return lax.dot_general(a, b, (((0,), (0,)), ((), ())), preferred_element_type=F32)


def _ada_kernel(c_ref, w_ref, b_ref, o_ref):
    s = _silu(c_ref[...])
    o_ref[...] = jnp.dot(s, w_ref[...], preferred_element_type=F32,
                         precision=lax.Precision.HIGHEST) + b_ref[...]


def ada_modulation(cc, w, b):
    rows, d = cc.shape
    n = w.shape[1]
    tn = _tile(n, (1024, 512, 256, 128))
    return pl.pallas_call(
        _ada_kernel,
        out_shape=jax.ShapeDtypeStruct((rows, n), F32),
        grid=(n // tn,),
        in_specs=[pl.BlockSpec((rows, d), lambda j: (0, 0)),
                  pl.BlockSpec((d, tn), lambda j: (0, j)),
                  pl.BlockSpec((1, tn), lambda j: (0, j))],
        out_specs=pl.BlockSpec((rows, tn), lambda j: (0, j)),
        compiler_params=_params("arbitrary"),
        name="ada_modulation",
    )(cc, w, b.reshape(1, n))


def _normmod_kernel(x_ref, w_ref, sh_ref, sc_ref, o_ref):
    x = x_ref[0]
    ms = jnp.mean(x * x, axis=-1, keepdims=True)
    y = x * lax.rsqrt(ms + EPS) * w_ref[...]
    o_ref[0] = (y * (1.0 + sc_ref[0]) + sh_ref[0]).astype(o_ref.dtype)


def norm_modulate(x, w, shift, scale, out_dtype=BF16):
    b, t, d = x.shape
    tm = _tile(t, (512, 256, 128))
    bm = shift.shape[0]
    mod_map = (lambda i, j: (i, 0, 0)) if bm == b else (lambda i, j: (0, 0, 0))
    return pl.pallas_call(
        _normmod_kernel,
        out_shape=jax.ShapeDtypeStruct((b, t, d), out_dtype),
        grid=(b, t // tm),
        in_specs=[pl.BlockSpec((1, tm, d), lambda i, j: (i, j, 0)),
                  pl.BlockSpec((1, d), lambda i, j: (0, 0)),
                  pl.BlockSpec((1, 1, d), mod_map),
                  pl.BlockSpec((1, 1, d), mod_map)],
        out_specs=pl.BlockSpec((1, tm, d), lambda i, j: (i, j, 0)),
        compiler_params=_params("arbitrary", "arbitrary"),
        name="norm_modulate",
    )(x, w.reshape(1, d), shift, scale)


def _rmsnorm_kernel(x_ref, w_ref, o_ref):
    x = x_ref[...]
    ms = jnp.mean(x * x, axis=-1, keepdims=True)
    o_ref[...] = x * lax.rsqrt(ms + EPS) * w_ref[...]


def rms_norm_rows(x, w):
    m, d = x.shape
    tm = _tile(m, (512, 256, 128))
    return pl.pallas_call(
        _rmsnorm_kernel,
        out_shape=jax.ShapeDtypeStruct((m, d), F32),
        grid=(m // tm,),
        in_specs=[pl.BlockSpec((tm, d), lambda i: (i, 0)),
                  pl.BlockSpec((1, d), lambda i: (0, 0))],
        out_specs=pl.BlockSpec((tm, d), lambda i: (i, 0)),
        compiler_params=_params("arbitrary"),
        name="final_rms_norm",
    )(x, w.reshape(1, d))


def _mm_kernel(a_ref, w_ref, o_ref, *, act):
    acc = _dot(a_ref[...], w_ref[...])
    if act == "sigmoid":
        acc = _sigmoid(acc)
    o_ref[...] = acc.astype(o_ref.dtype)


def matmul(a, w, out_dtype, act=None):
    m, k = a.shape
    n = w.shape[1]
    tm = _tile(m, (1024, 512, 256, 128))
    tn = _tile(n, (1024, 512, 256, 128))
    return pl.pallas_call(
        functools.partial(_mm_kernel, act=act),
        out_shape=jax.ShapeDtypeStruct((m, n), out_dtype),
        grid=(m // tm, n // tn),
        in_specs=[pl.BlockSpec((tm, k), lambda i, j: (i, 0)),
                  pl.BlockSpec((k, tn), lambda i, j: (0, j))],
        out_specs=pl.BlockSpec((tm, tn), lambda i, j: (i, j)),
        compiler_params=_params("arbitrary", "arbitrary"),
        name="matmul_" + (act or "plain"),
    )(a, w)


def _mm_res_kernel(a_ref, w_ref, r_ref, m_ref, o_ref):
    o_ref[...] = r_ref[...] + m_ref[0] * _dot(a_ref[...], w_ref[...])


def matmul_residual(a, w, res, mod):
    m, k = a.shape
    n = w.shape[1]
    rows_per_mod = m // mod.shape[0]
    tm = _tile(rows_per_mod, (1024, 512, 256, 128))
    tn = _tile(n, (1024, 512, 256, 128))
    return pl.pallas_call(
        _mm_res_kernel,
        out_shape=jax.ShapeDtypeStruct((m, n), F32),
        grid=(m // tm, n // tn),
        in_specs=[pl.BlockSpec((tm, k), lambda i, j: (i, 0)),
                  pl.BlockSpec((k, tn), lambda i, j: (0, j)),
                  pl.BlockSpec((tm, tn), lambda i, j: (i, j)),
                  pl.BlockSpec((1, 1, tn), lambda i, j: (i * tm // rows_per_mod, 0, j))],
        out_specs=pl.BlockSpec((tm, tn), lambda i, j: (i, j)),
        compiler_params=_params("arbitrary", "arbitrary"),
        name="matmul_residual",
    )(a, w, res, mod)


def _merge_kernel(ya_ref, yb_ref, yc_ref, g0_ref, g1_ref, g2_ref, wb_ref, o_ref):
    acc = g0_ref[...].astype(F32) * _dot(ya_ref[...], wb_ref[0])
    acc = acc + g1_ref[...].astype(F32) * _dot(yb_ref[...], wb_ref[1])
    acc = acc + g2_ref[...].astype(F32) * _dot(yc_ref[...], wb_ref[2])
    o_ref[...] = acc.astype(o_ref.dtype)


def merge_branches(ya, yb, yc, gates, wb):
    m, kb = ya.shape
    d = wb.shape[2]
    tm = _tile(m, (1024, 512, 256, 128))
    tn = _tile(d, (512, 256, 128))
    nj = d // tn
    y_spec = pl.BlockSpec((tm, kb), lambda i, j: (i, 0))

    def g_spec(nb):
        return pl.BlockSpec((tm, tn), lambda i, j: (i, nb * nj + j))

    return pl.pallas_call(
        _merge_kernel,
        out_shape=jax.ShapeDtypeStruct((m, d), BF16),
        grid=(m // tm, nj),
        in_specs=[y_spec, y_spec, y_spec, g_spec(0), g_spec(1), g_spec(2),
                  pl.BlockSpec((N_BRANCHES, kb, tn), lambda i, j: (0, 0, j))],
        out_specs=pl.BlockSpec((tm, tn), lambda i, j: (i, j)),
        compiler_params=_params("arbitrary", "arbitrary"),
        name="merge_branches",
    )(ya, yb, yc, gates, gates, gates, wb)


def _ffn_kernel(h_ref, wg_ref, wu_ref, wd_ref, x_ref, m_ref, o_ref, acc_ref):
    f = pl.program_id(1)

    @pl.when(f == 0)
    def _():
        acc_ref[...] = jnp.zeros_like(acc_ref)

    h = h_ref[...]
    a = _silu(_dot(h, wg_ref[...])) * _dot(h, wu_ref[...])
    acc_ref[...] += _dot(a.astype(BF16), wd_ref[...])

    @pl.when(f == pl.num_programs(1) - 1)
    def _():
        o_ref[...] = x_ref[...] + m_ref[0] * acc_ref[...]


def swiglu_residual(h, wg, wu, wd, x, mod):
    m, d = h.shape
    ff = wg.shape[1]
    rows_per_mod = m // mod.shape[0]
    tm = _tile(rows_per_mod, (512, 256, 128))
    tf = _tile(ff, (512, 256, 128))
    return pl.pallas_call(
        _ffn_kernel,
        out_shape=jax.ShapeDtypeStruct((m, d), F32),
        grid=(m // tm, ff // tf),
        in_specs=[pl.BlockSpec((tm, d), lambda i, f: (i, 0)),
                  pl.BlockSpec((d, tf), lambda i, f: (0, f)),
                  pl.BlockSpec((d, tf), lambda i, f: (0, f)),
                  pl.BlockSpec((tf, d), lambda i, f: (f, 0)),
                  pl.BlockSpec((tm, d), lambda i, f: (i, 0)),
                  pl.BlockSpec((1, 1, d), lambda i, f: (i * tm // rows_per_mod, 0, 0))],
        out_specs=pl.BlockSpec((tm, d), lambda i, f: (i, 0)),
        scratch_shapes=[pltpu.VMEM((tm, d), F32)],
        compiler_params=_params("arbitrary", "arbitrary"),
        name="swiglu_residual",
    )(h, wg, wu, wd, x, mod)


MOE_TILE = 512
ROUTE_E0, ROUTE_E1, ROUTE_W0, ROUTE_W1, ROUTE_P0, ROUTE_P1 = range(6)


def _router_kernel(x_ref, w_ref, sh_ref, sc_ref, rw_ref, h_ref, r_ref, cnt_ref, run_ref):
    @pl.when((pl.program_id(0) == 0) & (pl.program_id(1) == 0))
    def _():
        run_ref[...] = jnp.zeros_like(run_ref)

    x = x_ref[0]
    ms = jnp.mean(x * x, axis=-1, keepdims=True)
    h = x * lax.rsqrt(ms + EPS) * w_ref[...] * (1.0 + sc_ref[0]) + sh_ref[0]
    h_ref[0] = h
    logits = jnp.dot(h, rw_ref[...], preferred_element_type=F32, precision=lax.Precision.HIGHEST)
    lane = lax.broadcasted_iota(jnp.int32, logits.shape, 1)
    logits = jnp.where(lane < N_EXPERTS, logits, -jnp.inf)
    m1 = jnp.max(logits, axis=-1, keepdims=True)
    i1 = jnp.min(jnp.where(logits == m1, lane, LANES), axis=-1, keepdims=True)
    rest = jnp.where(lane == i1, -jnp.inf, logits)
    m2 = jnp.max(rest, axis=-1, keepdims=True)
    i2 = jnp.min(jnp.where(rest == m2, lane, LANES), axis=-1, keepdims=True)
    e2 = jnp.exp(m2 - m1)
    den = 1.0 + e2
    sel1 = lane == i1
    sel2 = lane == i2
    onehot = jnp.where(sel1 | sel2, 1.0, 0.0)
    tm = onehot.shape[0]
    tri = jnp.where(lax.broadcasted_iota(jnp.int32, (tm, tm), 0) > lax.broadcasted_iota(jnp.int32, (tm, tm), 1),
                    1.0, 0.0).astype(BF16)
    ranks = _dot(tri, onehot.astype(BF16)) + run_ref[...]
    p1 = jnp.sum(jnp.where(sel1, ranks, 0.0), axis=-1, keepdims=True)
    p2 = jnp.sum(jnp.where(sel2, ranks, 0.0), axis=-1, keepdims=True)
    run_ref[...] += jnp.sum(onehot, axis=0, keepdims=True)
    cnt_ref[...] = jnp.broadcast_to(run_ref[...], cnt_ref.shape)
    rec = jnp.where(lane == ROUTE_E0, i1.astype(F32), 0.0)
    rec = jnp.where(lane == ROUTE_E1, i2.astype(F32), rec)
    rec = jnp.where(lane == ROUTE_W0, 1.0 / den, rec)
    rec = jnp.where(lane == ROUTE_W1, e2 / den, rec)
    rec = jnp.where(lane == ROUTE_P0, p1, rec)
    rec = jnp.where(lane == ROUTE_P1, p2, rec)
    r_ref[0] = rec


def route_top2(x, w, shift, scale, router_w):
    b, t, d = x.shape
    tm = _tile(t, (512, 256, 128))
    rw = jnp.zeros((d, LANES), F32).at[:, :N_EXPERTS].set(router_w)
    return pl.pallas_call(
        _router_kernel,
        out_shape=(jax.ShapeDtypeStruct((b, t, d), F32), jax.ShapeDtypeStruct((b, t, LANES), F32),
                   jax.ShapeDtypeStruct((SUBLANES, LANES), F32)),
        grid=(b, t // tm),
        in_specs=[pl.BlockSpec((1, tm, d), lambda i, j: (i, j, 0)),
                  pl.BlockSpec((1, d), lambda i, j: (0, 0)),
                  pl.BlockSpec((1, 1, d), lambda i, j: (i, 0, 0)),
                  pl.BlockSpec((1, 1, d), lambda i, j: (i, 0, 0)),
                  pl.BlockSpec((d, LANES), lambda i, j: (0, 0))],
        out_specs=(pl.BlockSpec((1, tm, d), lambda i, j: (i, j, 0)),
                   pl.BlockSpec((1, tm, LANES), lambda i, j: (i, j, 0)),
                   pl.BlockSpec((SUBLANES, LANES), lambda i, j: (0, 0))),
        scratch_shapes=[pltpu.VMEM((1, LANES), F32)],
        compiler_params=_params("arbitrary", "arbitrary"),
        name="moe_router",
    )(x, w.reshape(1, d), shift, scale, rw)


def _row_copy(src, src_row, dst, dst_row, sem):
    return pltpu.make_async_copy(src.at[pl.ds(src_row, 1)], dst.at[pl.ds(dst_row, 1)], sem)


def _dispatch_kernel(d0_ref, d1_ref, h_ref, init_hbm, hs_hbm, sem, *, tm):
    del init_hbm
    base = pl.program_id(0) * tm

    def issue(r, carry):
        t = base + r
        _row_copy(h_ref, r, hs_hbm, d0_ref[t], sem).start()
        _row_copy(h_ref, r, hs_hbm, d1_ref[t], sem).start()
        return carry

    def drain(r, carry):
        _row_copy(h_ref, 0, hs_hbm, 0, sem).wait()
        _row_copy(h_ref, 0, hs_hbm, 0, sem).wait()
        return carry

    lax.fori_loop(0, tm, issue, 0)
    lax.fori_loop(0, tm, drain, 0)


def moe_dispatch(h, dest0, dest1, n_rows):
    m, d = h.shape
    tm = _tile(m, (256, 128))
    any_spec = pl.BlockSpec(memory_space=pl.ANY)
    return pl.pallas_call(
        functools.partial(_dispatch_kernel, tm=tm),
        out_shape=jax.ShapeDtypeStruct((n_rows, d), F32),
        grid_spec=pltpu.PrefetchScalarGridSpec(
            num_scalar_prefetch=2, grid=(m // tm,),
            in_specs=[pl.BlockSpec((tm, d), lambda i, d0, d1: (i, 0)), any_spec], out_specs=any_spec,
            scratch_shapes=[pltpu.SemaphoreType.DMA]),
        input_output_aliases={3: 0},
        compiler_params=_params("arbitrary"),
        name="moe_dispatch",
    )(dest0, dest1, h, jnp.zeros((n_rows, d), F32))


def _grouped_ffn_kernel(te_ref, nu_ref, hs_ref, wg_ref, wu_ref, wd_ref, o_ref, acc_ref, hb_ref):
    del te_ref
    i = pl.program_id(0)
    f = pl.program_id(1)

    @pl.when(f == 0)
    def _():
        acc_ref[...] = jnp.zeros_like(acc_ref)
        hb_ref[...] = hs_ref[...].astype(BF16)

    @pl.when(i < nu_ref[0])
    def _():
        h = hb_ref[...]
        a = _silu(_dot(h, wg_ref[0])) * _dot(h, wu_ref[0])
        acc_ref[...] += _dot(a.astype(BF16), wd_ref[0])

    @pl.when(f == pl.num_programs(1) - 1)
    def _():
        o_ref[...] = acc_ref[...]


def moe_grouped_swiglu(hs, tile_expert, n_used, wg, wu, wd):
    p, d = hs.shape
    ff = wg.shape[2]
    tm = MOE_TILE
    tf = _tile(ff, (512, 256, 128))

    def fblock(i, f, nu):
        return jnp.where(i < nu[0], f, 0)

    return pl.pallas_call(
        _grouped_ffn_kernel,
        out_shape=jax.ShapeDtypeStruct((p, d), F32),
        grid_spec=pltpu.PrefetchScalarGridSpec(
            num_scalar_prefetch=2, grid=(p // tm, ff // tf),
            in_specs=[pl.BlockSpec((tm, d), lambda i, f, te, nu: (i, 0)),
                      pl.BlockSpec((1, d, tf), lambda i, f, te, nu: (te[i], 0, fblock(i, f, nu))),
                      pl.BlockSpec((1, d, tf), lambda i, f, te, nu: (te[i], 0, fblock(i, f, nu))),
                      pl.BlockSpec((1, tf, d), lambda i, f, te, nu: (te[i], fblock(i, f, nu), 0))],
            out_specs=pl.BlockSpec((tm, d), lambda i, f, te, nu: (i, 0)),
            scratch_shapes=[pltpu.VMEM((tm, d), F32), pltpu.VMEM((tm, d), BF16)]),
        compiler_params=_params("arbitrary", "arbitrary"),
        name="moe_grouped_swiglu",
    )(tile_expert, n_used, hs, wg, wu, wd)


def _combine_kernel(d0_ref, d1_ref, ys_hbm, r_ref, x_ref, m_ref, o_ref, buf_ref, sem, *, tm):
    base = pl.program_id(0) * tm

    def issue(r, carry):
        t = base + r
        _row_copy(ys_hbm, d0_ref[t], buf_ref.at[0], r, sem).start()
        _row_copy(ys_hbm, d1_ref[t], buf_ref.at[1], r, sem).start()
        return carry

    def drain(r, carry):
        _row_copy(ys_hbm, 0, buf_ref.at[0], 0, sem).wait()
        _row_copy(ys_hbm, 0, buf_ref.at[1], 0, sem).wait()
        return carry

    lax.fori_loop(0, tm, issue, 0)
    lax.fori_loop(0, tm, drain, 0)
    rec = r_ref[...]
    lane = lax.broadcasted_iota(jnp.int32, rec.shape, 1)
    w0 = jnp.sum(jnp.where(lane == ROUTE_W0, rec, 0.0), axis=-1, keepdims=True)
    w1 = jnp.sum(jnp.where(lane == ROUTE_W1, rec, 0.0), axis=-1, keepdims=True)
    o_ref[...] = x_ref[...] + m_ref[0] * (w0 * buf_ref[0] + w1 * buf_ref[1])


def moe_combine(ys, dest0, dest1, route, x, mod):
    m, d = x.shape
    rows_per_mod = m // mod.shape[0]
    tm = _tile(rows_per_mod, (256, 128))
    return pl.pallas_call(
        functools.partial(_combine_kernel, tm=tm),
        out_shape=jax.ShapeDtypeStruct((m, d), F32),
        grid_spec=pltpu.PrefetchScalarGridSpec(
            num_scalar_prefetch=2, grid=(m // tm,),
            in_specs=[pl.BlockSpec(memory_space=pl.ANY),
                      pl.BlockSpec((tm, LANES), lambda i, d0, d1: (i, 0)),
                      pl.BlockSpec((tm, d), lambda i, d0, d1: (i, 0)),
                      pl.BlockSpec((1, 1, d), lambda i, d0, d1: (i * tm // rows_per_mod, 0, 0))],
            out_specs=pl.BlockSpec((tm, d), lambda i, d0, d1: (i, 0)),
            scratch_shapes=[pltpu.VMEM((2, tm, d), F32), pltpu.SemaphoreType.DMA]),
        compiler_params=_params("arbitrary"),
        name="moe_combine",
    )(dest0, dest1, ys, route, x, mod)


def moe_swiglu_residual(x3, norm_w, shift, scale, router_w, wg, wu, wd, mod):
    b, t, d = x3.shape
    m = b * t
    h, route, counts = route_top2(x3, norm_w, shift, scale, router_w)
    route = route.reshape(m, LANES)
    cnt = counts[0, :N_EXPERTS].astype(jnp.int32)
    padded = (cnt + MOE_TILE - 1) // MOE_TILE * MOE_TILE
    ends = jnp.cumsum(padded)
    starts = ends - padded
    e0 = route[:, ROUTE_E0].astype(jnp.int32)
    e1 = route[:, ROUTE_E1].astype(jnp.int32)
    dest0 = starts[e0] + route[:, ROUTE_P0].astype(jnp.int32)
    dest1 = starts[e1] + route[:, ROUTE_P1].astype(jnp.int32)
    n_rows = 2 * m + N_EXPERTS * MOE_TILE
    n_tiles = n_rows // MOE_TILE
    n_used = ends[-1] // MOE_TILE
    tile_ids = jnp.minimum(jnp.arange(n_tiles, dtype=jnp.int32), n_used - 1)
    tile_expert = jnp.sum((tile_ids[:, None] >= (ends // MOE_TILE)[None, :]).astype(jnp.int32), axis=1)
    tile_expert = jnp.minimum(tile_expert, N_EXPERTS - 1).astype(jnp.int32)
    hs = moe_dispatch(h.reshape(m, d), dest0, dest1, n_rows)
    ys = moe_grouped_swiglu(hs, tile_expert, n_used.reshape(1).astype(jnp.int32), wg, wu, wd)
    return moe_combine(ys, dest0, dest1, route, x3.reshape(m, d), mod)


def _hgrn2_kernel(q_ref, f_ref, v_ref, lb_ref, s0_ref, o_ref, sout_ref, st_ref, c_ref, k_ref, wblk_ref,
                  *, reverse, rows):
    j = pl.program_id(1)

    @pl.when(j == 0)
    def _():
        st_ref[...] = s0_ref[0]

    lb = lb_ref[...]
    fr = f_ref[0]
    log_sig = jnp.minimum(fr, 0.0) - jnp.log(1.0 + jnp.exp(-jnp.abs(fr)))
    a = jnp.log(lb)
    b = jnp.log(1.0 - lb) + log_sig
    log_f = jnp.maximum(a, b) + jnp.log(1.0 + jnp.exp(-jnp.abs(a - b)))
    k_ref[...] = (1.0 - lb) / (1.0 + jnp.exp(fr))
    rib = lax.broadcasted_iota(jnp.int32, fr.shape, 0) & (HG_BLOCK - 1)
    c = log_f
    sh = 1
    while sh < HG_BLOCK:
        if reverse:
            c = c + jnp.where(rib < HG_BLOCK - sh, pltpu.roll(c, rows - sh, axis=0), 0.0)
        else:
            c = c + jnp.where(rib >= sh, pltpu.roll(c, sh, axis=0), 0.0)
        sh *= 2
    c_ref[...] = c

    trow = lax.broadcasted_iota(jnp.int32, (HG_BLOCK, HG_DK), 0)
    blocks = list(range(rows // HG_BLOCK))
    if reverse:
        blocks = blocks[::-1]
    pw = 2 * HG_DK

    def head_pair(p, carry):
        off = pl.multiple_of(p * pw, pw)
        s_a = st_ref[2 * p]
        s_b = st_ref[2 * p + 1]
        wblk_ref[0:HG_DK, 0:HG_DK] = s_a.astype(BF16)
        wblk_ref[HG_DK:pw, HG_DK:pw] = s_b.astype(BF16)
        for blk in blocks:
            r0 = blk * HG_BLOCK
            qb = q_ref[0, r0:r0 + HG_BLOCK, pl.ds(off, pw)] * (HG_DK ** -0.5)
            vb = v_ref[0, r0:r0 + HG_BLOCK, pl.ds(off, pw)]
            kb = k_ref[r0:r0 + HG_BLOCK, pl.ds(off, pw)]
            cb = c_ref[r0:r0 + HG_BLOCK, pl.ds(off, pw)]
            o = _dot_nt((qb * jnp.exp(cb)).astype(BF16), wblk_ref[...])
            diag = []
            for hk in range(2):
                sl = slice(hk * HG_DK, (hk + 1) * HG_DK)
                qh, kh, vh, ch = qb[:, sl], kb[:, sl], vb[:, sl], cb[:, sl]
                oh = jnp.zeros((HG_BLOCK, HG_DK), F32)
                for jj in range(HG_BLOCK):
                    valid = (trow <= jj) if reverse else (trow >= jj)
                    e = jnp.exp(jnp.where(valid, ch - ch[jj:jj + 1], -jnp.inf)) * (qh * kh[jj:jj + 1])
                    oh = oh + jnp.sum(e, axis=-1, keepdims=True) * vh[jj:jj + 1]
                diag.append(oh)
            o_ref[0, r0:r0 + HG_BLOCK, pl.ds(off, pw)] = o + jnp.concatenate(diag, axis=1)
            c_tot = cb[0:1] if reverse else cb[HG_BLOCK - 1:HG_BLOCK]
            kt = (kb * jnp.exp(c_tot - cb)).astype(BF16)
            upd = _dot_tn(vb.astype(BF16), kt)
            grow = jnp.exp(c_tot)
            s_a = s_a * grow[:, :HG_DK] + upd[:HG_DK, :HG_DK]
            s_b = s_b * grow[:, HG_DK:] + upd[HG_DK:, HG_DK:]
            wblk_ref[0:HG_DK, 0:HG_DK] = s_a.astype(BF16)
            wblk_ref[HG_DK:pw, HG_DK:pw] = s_b.astype(BF16)
        st_ref[2 * p] = s_a
        st_ref[2 * p + 1] = s_b
        return carry

    wblk_ref[...] = jnp.zeros_like(wblk_ref)
    lax.fori_loop(0, HG_HEADS // 2, head_pair, 0)

    @pl.when(j == pl.num_programs(1) - 1)
    def _():
        sout_ref[0] = st_ref[...]


def hgrn2_scan(p_hg, lb, s0, reverse):
    b, t, _ = p_hg.shape
    w = HG_WIDTH
    rows = _tile(t, (64, 32, 16))
    nt = t // rows
    tmap = (lambda j: nt - 1 - j) if reverse else (lambda j: j)
    fcol = 2 if reverse else 1
    return pl.pallas_call(
        functools.partial(_hgrn2_kernel, reverse=reverse, rows=rows),
        out_shape=(jax.ShapeDtypeStruct((b, t, w), F32),
                   jax.ShapeDtypeStruct(s0.shape, F32)),
        grid=(b, nt),
        in_specs=[pl.BlockSpec((1, rows, w), lambda i, j: (i, tmap(j), 0)),
                  pl.BlockSpec((1, rows, w), lambda i, j: (i, tmap(j), fcol)),
                  pl.BlockSpec((1, rows, w), lambda i, j: (i, tmap(j), 3)),
                  pl.BlockSpec((1, w), lambda i, j: (0, 0)),
                  pl.BlockSpec((1, HG_HEADS, HG_DK, HG_DK), lambda i, j: (i, 0, 0, 0))],
        out_specs=(pl.BlockSpec((1, rows, w), lambda i, j: (i, tmap(j), 0)),
                   pl.BlockSpec((1, HG_HEADS, HG_DK, HG_DK), lambda i, j: (i, 0, 0, 0))),
        scratch_shapes=[pltpu.VMEM((HG_HEADS, HG_DK, HG_DK), F32),
                        pltpu.VMEM((rows, w), F32),
                        pltpu.VMEM((rows, w), F32),
                        pltpu.VMEM((2 * HG_DK, 2 * HG_DK), BF16)],
        compiler_params=_params("arbitrary", "arbitrary"),
        name="hgrn2_scan_bwd" if reverse else "hgrn2_scan_fwd",
    )(p_hg, p_hg, p_hg, lb.reshape(1, w), s0)


def _hg_readout_kernel(of_ref, ob_ref, g_ref, w_ref, y_ref):
    for h in range(HG_HEADS):
        sl = slice(h * HG_DK, (h + 1) * HG_DK)
        o = of_ref[:, sl] + ob_ref[:, sl]
        ms = jnp.mean(o * o, axis=-1, keepdims=True)
        y = o * lax.rsqrt(ms + EPS) * w_ref[:, sl]
        y_ref[:, sl] = (y * _silu(g_ref[:, sl])).astype(y_ref.dtype)


def hgrn2_readout(o_f, o_b, p_hg, norm_w):
    m, w = o_f.shape
    tm = _tile(m, (512, 256, 128))
    spec = pl.BlockSpec((tm, w), lambda i: (i, 0))
    return pl.pallas_call(
        _hg_readout_kernel,
        out_shape=jax.ShapeDtypeStruct((m, w), BF16),
        grid=(m // tm,),
        in_specs=[spec, spec, pl.BlockSpec((tm, w), lambda i: (i, 4)),
                  pl.BlockSpec((1, w), lambda i: (0, 0))],
        out_specs=spec,
        compiler_params=_params("arbitrary"),
        name="hgrn2_readout",
    )(o_f, o_b, p_hg, norm_w.reshape(1, w))


def _conv_kernel(prev_ref, cur_ref, next_ref, w_ref, b_ref, o_ref, *, tm):
    i = pl.program_id(1)
    prev = jnp.where(i > 0, prev_ref[0], 0.0)
    nxt = jnp.where(i < pl.num_programs(1) - 1, next_ref[0], 0.0)
    xe = jnp.concatenate([prev, cur_ref[0], nxt], axis=0)
    ext = tm + 2 * SUBLANES
    acc = jnp.zeros((tm, xe.shape[1]), F32) + b_ref[...]
    for k in range(SSM_CONV):
        sh = (SSM_CONV // 2 - k) % ext
        xs = xe if sh == 0 else pltpu.roll(xe, sh, axis=0)
        acc = acc + w_ref[k:k + 1, :] * xs[SUBLANES:SUBLANES + tm]
    o_ref[0] = _silu(acc)


def ssd_conv_silu(p_ssd, conv_w, conv_b):
    b, t, _ = p_ssd.shape
    ch = SSM_CONV_CH
    tm = _tile(t, (512, 256, 128))
    tc = 1024
    c0 = SSM_WIDTH // tc
    hb = tm // SUBLANES
    nh = t // SUBLANES
    return pl.pallas_call(
        functools.partial(_conv_kernel, tm=tm),
        out_shape=jax.ShapeDtypeStruct((b, t, ch), F32),
        grid=(b, t // tm, ch // tc),
        in_specs=[pl.BlockSpec((1, SUBLANES, tc), lambda i, j, cj: (i, jnp.maximum(j * hb - 1, 0), c0 + cj)),
                  pl.BlockSpec((1, tm, tc), lambda i, j, cj: (i, j, c0 + cj)),
                  pl.BlockSpec((1, SUBLANES, tc), lambda i, j, cj: (i, jnp.minimum((j + 1) * hb, nh - 1), c0 + cj)),
                  pl.BlockSpec((SUBLANES, tc), lambda i, j, cj: (0, cj)),
                  pl.BlockSpec((1, tc), lambda i, j, cj: (0, cj))],
        out_specs=pl.BlockSpec((1, tm, tc), lambda i, j, cj: (i, j, cj)),
        compiler_params=_params("arbitrary", "arbitrary", "arbitrary"),
        name="ssd_conv_silu",
    )(p_ssd, p_ssd, p_ssd,
      jnp.zeros((SUBLANES, ch), F32).at[:SSM_CONV].set(conv_w), conv_b.reshape(1, ch))


def _split3(x):
    hi = x.astype(BF16)
    r1 = x - hi.astype(F32)
    mid = r1.astype(BF16)
    lo = (r1 - mid.astype(F32)).astype(BF16)
    return hi, mid, lo


def _pick_lanes(parts, sel):
    return _dot(parts[0], sel) + _dot(parts[1], sel) + _dot(parts[2], sel)


def _ssd_kernel(x_ref, b_ref, c_ref, dt_ref, prm_ref, selh_ref, selp_ref, h0_ref, y_ref, hout_ref, hst_ref,
                *, reverse):
    j = pl.program_id(1)
    cl = SSM_CHUNK
    hd = SSM_HEAD_DIM

    @pl.when(j == 0)
    def _():
        hst_ref[...] = h0_ref[0]

    prm = prm_ref[...]
    dtr = dt_ref[0] + prm[0:1]
    dt = jnp.maximum(dtr, 0.0) + jnp.log(1.0 + jnp.exp(-jnp.abs(dtr)))
    a = dt * prm[1:2]
    row = lax.broadcasted_iota(jnp.int32, a.shape, 0)
    ac = a
    sh = 1
    while sh < cl:
        if reverse:
            ac = ac + jnp.where(row < cl - sh, pltpu.roll(ac, cl - sh, axis=0), 0.0)
        else:
            ac = ac + jnp.where(row >= sh, pltpu.roll(ac, sh, axis=0), 0.0)
        sh *= 2
    ac_t = ac.T
    ac_parts = _split3(ac)
    ac_head = _pick_lanes(ac_parts, selh_ref[...])
    ac_x = _pick_lanes(ac_parts, selp_ref[...])
    dt_x = _pick_lanes(_split3(dt), selp_ref[...])
    d_x = _pick_lanes(_split3(prm), selp_ref[...])[2:3]
    tot_x = ac_x[0:1] if reverse else ac_x[cl - 1:cl]
    xs = x_ref[0]
    xdt = xs * dt_x
    y_skip = xs * d_x
    exp_ac = jnp.exp(ac_x)
    xdec = (xdt * jnp.exp(tot_x - ac_x)).astype(BF16)
    exp_tot = jnp.exp(tot_x)
    xdt = xdt.astype(BF16)

    ti = lax.broadcasted_iota(jnp.int32, (cl, cl), 0)
    si = lax.broadcasted_iota(jnp.int32, (cl, cl), 1)
    mask = (ti <= si) if reverse else (ti >= si)
    first_head = si < hd
    zero = jnp.zeros((), BF16)
    lane0 = SSM_HEADS if reverse else 0
    pairs_per_group = SSM_HEADS // SSM_GROUPS // 2
    for g in range(SSM_GROUPS):
        bg = b_ref[0, :, g * SSM_STATE:(g + 1) * SSM_STATE].astype(BF16)
        cg = c_ref[0, :, g * SSM_STATE:(g + 1) * SSM_STATE].astype(BF16)
        cb = _dot_nt(cg, bg)
        for r in range(pairs_per_group):
            p = g * pairs_per_group + r
            cols = slice(p * 2 * hd, (p + 1) * 2 * hd)
            xp = xdt[:, cols]
            y = None
            for k, keep in ((0, first_head), (1, ~first_head)):
                h = 2 * p + k
                arow = ac_t[lane0 + h:lane0 + h + 1, :]
                acol = ac_head[:, h * LANES:(h + 1) * LANES]
                w = cb * jnp.exp(jnp.where(mask, acol - arow, -jnp.inf))
                yk = _dot(w.astype(BF16), jnp.where(keep, xp, zero))
                y = yk if y is None else y + yk
            hp = hst_ref[p]
            y = y + exp_ac[:, cols] * _dot_nt(cg, hp.astype(BF16)) + y_skip[:, cols]
            grow = jnp.where(ti < hd, exp_tot[:, p * 2 * hd:p * 2 * hd + 1], exp_tot[:, (p + 1) * 2 * hd - 1:(p + 1) * 2 * hd])
            hst_ref[p] = hp * grow + _dot_tn(xdec[:, cols], bg)
            y_ref[0, :, cols] = y

    @pl.when(j == pl.num_programs(1) - 1)
    def _():
        hout_ref[0] = hst_ref[...]


def ssd_scan(xbc, dt_raw, prm, h0, reverse):
    b, t, _ = xbc.shape
    cl = SSM_CHUNK
    nt = t // cl
    gw = SSM_GROUPS * SSM_STATE
    tmap = (lambda j: nt - 1 - j) if reverse else (lambda j: j)
    lane0 = SSM_HEADS if reverse else 0
    src = jnp.arange(LANES, dtype=jnp.int32)[:, None]
    sel_head = (src == lane0 + jnp.arange(SSM_HEADS * LANES, dtype=jnp.int32)[None, :] // LANES).astype(BF16)
    sel_x = (src == lane0 + jnp.arange(SSM_WIDTH, dtype=jnp.int32)[None, :] // SSM_HEAD_DIM).astype(BF16)
    st_spec = pl.BlockSpec((1,) + h0.shape[1:], lambda i, j: (i, 0, 0, 0))
    return pl.pallas_call(
        functools.partial(_ssd_kernel, reverse=reverse),
        out_shape=(jax.ShapeDtypeStruct((b, t, SSM_WIDTH), F32),
                   jax.ShapeDtypeStruct(h0.shape, F32)),
        grid=(b, nt),
        in_specs=[pl.BlockSpec((1, cl, SSM_WIDTH), lambda i, j: (i, tmap(j), 0)),
                  pl.BlockSpec((1, cl, gw), lambda i, j: (i, tmap(j), SSM_WIDTH // gw)),
                  pl.BlockSpec((1, cl, gw), lambda i, j: (i, tmap(j), SSM_WIDTH // gw + 1)),
                  pl.BlockSpec((1, cl, LANES), lambda i, j: (i, tmap(j), 0)),
                  pl.BlockSpec((SUBLANES, LANES), lambda i, j: (0, 0)),
                  pl.BlockSpec(sel_head.shape, lambda i, j: (0, 0)),
                  pl.BlockSpec(sel_x.shape, lambda i, j: (0, 0)),
                  st_spec],
        out_specs=(pl.BlockSpec((1, cl, SSM_WIDTH), lambda i, j: (i, tmap(j), 0)), st_spec),
        scratch_shapes=[pltpu.VMEM(h0.shape[1:], F32)],
        compiler_params=_params("arbitrary", "arbitrary"),
        name="ssd_scan_bwd" if reverse else "ssd_scan_fwd",
    )(xbc, xbc, xbc, dt_raw, prm, sel_head, sel_x, h0)


def _ssd_readout_kernel(yf_ref, yb_ref, z_ref, w_ref, o_ref):
    gw = SSM_WIDTH // SSM_GROUPS
    for g in range(SSM_GROUPS):
        sl = slice(g * gw, (g + 1) * gw)
        y = (yf_ref[:, sl] + yb_ref[:, sl]) * _silu(z_ref[:, sl])
        ms = jnp.mean(y * y, axis=-1, keepdims=True)
        o_ref[:, sl] = (y * lax.rsqrt(ms + EPS) * w_ref[:, sl]).astype(o_ref.dtype)


def ssd_readout(y_f, y_b, p_ssd, norm_w):
    m, w = y_f.shape
    tm = _tile(m, (512, 256, 128))
    spec = pl.BlockSpec((tm, w), lambda i: (i, 0))
    return pl.pallas_call(
        _ssd_readout_kernel,
        out_shape=jax.ShapeDtypeStruct((m, w), BF16),
        grid=(m // tm,),
        in_specs=[spec, spec, spec, pl.BlockSpec((1, w), lambda i: (0, 0))],
        out_specs=spec,
        compiler_params=_params("arbitrary"),
        name="ssd_readout",
    )(y_f, y_b, p_ssd, norm_w.reshape(1, w))


def _attprep_kernel(p_ref, qw_ref, kw_ref, cos_ref, sin_ref, q_ref, k_ref, v_ref, *, rope):
    lane = lax.broadcasted_iota(jnp.int32, (p_ref.shape[0], ATT_HEAD_DIM), 1)
    first = (lane % (ATT_HEAD_DIM // 2)) < (ATT_HEAD_DIM // 4)
    for h in range(ATT_HEADS + ATT_KV_HEADS):
        xh = p_ref[:, h * ATT_HEAD_DIM:(h + 1) * ATT_HEAD_DIM]
        w = qw_ref[...] if h < ATT_HEADS else kw_ref[...]
        ms = jnp.mean(xh * xh, axis=-1, keepdims=True)
        y = xh * lax.rsqrt(ms + EPS) * w
        if rope:
            partner = jnp.where(first, pltpu.roll(y, 3 * ATT_HEAD_DIM // 4, axis=1),
                                pltpu.roll(y, ATT_HEAD_DIM // 4, axis=1))
            y = y * cos_ref[...] + partner * sin_ref[...]
        if h < ATT_HEADS:
            q_ref[:, h * ATT_HEAD_DIM:(h + 1) * ATT_HEAD_DIM] = (y * ATT_Q_SCALE).astype(q_ref.dtype)
        else:
            hk = h - ATT_HEADS
            k_ref[:, hk * ATT_HEAD_DIM:(hk + 1) * ATT_HEAD_DIM] = y.astype(k_ref.dtype)
            v0 = ATT_WIDTH + ATT_KV_WIDTH + hk * ATT_HEAD_DIM
            v_ref[:, 2 * hk * ATT_HEAD_DIM:(2 * hk + 1) * ATT_HEAD_DIM] = p_ref[:, v0:v0 + ATT_HEAD_DIM].astype(v_ref.dtype)
            v_ref[:, (2 * hk + 1) * ATT_HEAD_DIM:(2 * hk + 2) * ATT_HEAD_DIM] = jnp.ones(
                (p_ref.shape[0], ATT_HEAD_DIM), v_ref.dtype)


def attention_prep(p_att, q_norm_w, k_norm_w, cos, sin, seq, rope):
    m, wtot = p_att.shape
    tm = _tile(seq, (512, 256, 128))
    npos = seq // tm
    tab = pl.BlockSpec((tm, ATT_HEAD_DIM), (lambda i: (i % npos, 0)) if rope else (lambda i: (0, 0)))
    wspec = pl.BlockSpec((1, ATT_HEAD_DIM), lambda i: (0, 0))
    return pl.pallas_call(
        functools.partial(_attprep_kernel, rope=rope),
        out_shape=(jax.ShapeDtypeStruct((m, ATT_WIDTH), BF16),
                   jax.ShapeDtypeStruct((m, ATT_KV_WIDTH), BF16),
                   jax.ShapeDtypeStruct((m, 2 * ATT_KV_WIDTH), BF16)),
        grid=(m // tm,),
        in_specs=[pl.BlockSpec((tm, wtot), lambda i: (i, 0)), wspec, wspec, tab, tab],
        out_specs=(pl.BlockSpec((tm, ATT_WIDTH), lambda i: (i, 0)),
                   pl.BlockSpec((tm, ATT_KV_WIDTH), lambda i: (i, 0)),
                   pl.BlockSpec((tm, 2 * ATT_KV_WIDTH), lambda i: (i, 0))),
        compiler_params=_params("arbitrary"),
        name="attention_prep_rope" if rope else "attention_prep",
    )(p_att, q_norm_w.reshape(1, -1), k_norm_w.reshape(1, -1), cos, sin)


def _attn_kernel(q_ref, k_ref, v_ref, o_ref, *, tk, nk):
    tq = q_ref.shape[1]
    qs = [q_ref[0, :, g * ATT_HEAD_DIM:(g + 1) * ATT_HEAD_DIM] for g in range(ATT_GROUP)]

    def body(c, carry):
        off = pl.multiple_of(c * tk, tk)
        kc = k_ref[0, pl.ds(off, tk), :]
        vc = v_ref[0, pl.ds(off, tk), :]
        out = []
        for g in range(ATT_GROUP):
            m, acc = carry[g]
            s = _dot_nt(qs[g], kc)
            mn = jnp.maximum(m, jnp.max(s, axis=-1, keepdims=True))
            p = jnp.exp2(s - mn)
            out.append((mn, jnp.exp2(m - mn) * acc + _dot(p.astype(BF16), vc)))
        return tuple(out)

    init = tuple((jnp.full((tq, 1), -jnp.inf, F32), jnp.zeros((tq, 2 * ATT_HEAD_DIM), F32))
                 for _ in range(ATT_GROUP))
    res = lax.fori_loop(0, nk, body, init)
    for g in range(ATT_GROUP):
        acc = res[g][1]
        o_ref[0, :, g * ATT_HEAD_DIM:(g + 1) * ATT_HEAD_DIM] = (
            acc[:, :ATT_HEAD_DIM] / acc[:, ATT_HEAD_DIM:]).astype(o_ref.dtype)


def gqa_attention(q, k, v):
    b, t, _ = q.shape
    s = k.shape[1]
    tq = _tile(t, (256, 128))
    tk = s if s <= 1024 else _tile(s, (2816, 1408, 768, 512, 384, 256, 128))
    gwid = ATT_GROUP * ATT_HEAD_DIM
    return pl.pallas_call(
        functools.partial(_attn_kernel, tk=tk, nk=s // tk),
        out_shape=jax.ShapeDtypeStruct(q.shape, BF16),
        grid=(b, ATT_KV_HEADS, t // tq),
        in_specs=[pl.BlockSpec((1, tq, gwid), lambda i, h, j: (i, j, h)),
                  pl.BlockSpec((1, s, ATT_HEAD_DIM), lambda i, h, j: (i, 0, h)),
                  pl.BlockSpec((1, s, 2 * ATT_HEAD_DIM), lambda i, h, j: (i, 0, h))],
        out_specs=pl.BlockSpec((1, tq, gwid), lambda i, h, j: (i, j, h)),
        compiler_params=_params("arbitrary", "arbitrary", "arbitrary"),
        name="gqa_attention",
    )(q, k, v)


def _rope_tables(n_tokens):
    rows = n_tokens // GRID_W
    row = jnp.repeat(jnp.arange(rows, dtype=F32), GRID_W)
    col = jnp.tile(jnp.arange(GRID_W, dtype=F32), rows)
    axis_dim = ATT_HEAD_DIM // 2
    inv = ROPE_THETA ** (-jnp.arange(0, axis_dim, 2, dtype=F32) / axis_dim)
    ar, ac = row[:, None] * inv, col[:, None] * inv
    cos = jnp.concatenate([jnp.cos(ar), jnp.cos(ar), jnp.cos(ac), jnp.cos(ac)], axis=-1)
    sin = jnp.concatenate([-jnp.sin(ar), jnp.sin(ar), -jnp.sin(ac), jnp.sin(ac)], axis=-1)
    return cos, sin


def kernel(x, c, ctx, c_ctx, ada_w, ada_b, norm_mix_w, norm_ffn_w, w_in, hg_lb_logits, hg_norm_w, ssm_conv_w, ssm_conv_b, ssm_dt_bias, ssm_a_log, ssm_d, ssm_norm_w, attn_q_norm_w, attn_k_norm_w, w_branch, w_out, ffn_w_gate, ffn_w_up, ffn_w_down, moe_router, moe_w_gate, moe_w_up, moe_w_down, final_norm_w):
    bsz, seq, d = x.shape
    clen = ctx.shape[1]
    depth = ada_w.shape[0]
    m_lat, m_ctx = bsz * seq, bsz * clen

    lb_all = jnp.cumsum(jax.nn.softmax(hg_lb_logits.astype(F32), axis=0), axis=0)
    lb_all = lb_all - lb_all[0]
    cond = jnp.zeros((SUBLANES, d), F32).at[:bsz].set(c).at[bsz].set(c_ctx)
    cos, sin = _rope_tables(seq)

    e_hg = 5 * HG_WIDTH
    e_ssd = e_hg + SSM_WIDTH + SSM_CONV_CH
    e_dt = e_ssd + 2 * SSM_HEADS
    e_att = e_dt + ATT_WIDTH + 2 * ATT_KV_WIDTH

    zeros_hg = jnp.zeros((bsz, HG_HEADS, HG_DK, HG_DK), F32)
    zeros_ssd = jnp.zeros((bsz, SSM_HEADS // 2, 2 * SSM_HEAD_DIM, SSM_STATE), F32)

    x2 = x.reshape(m_lat, d)
    xc2 = ctx.reshape(m_ctx, d)
    for l in range(depth):
        need_ctx = l < depth - 1
        mod = ada_modulation(cond, ada_w[l], ada_b[l])
        mod_lat = mod[:bsz].reshape(bsz, 6, 1, d)
        mod_ctx = mod[bsz:bsz + 1].reshape(1, 6, 1, d)
        ml = [mod_lat[:, n] for n in range(6)]
        mc = [mod_ctx[:, n] for n in range(6)]

        wl = w_in[l]
        w_hg = wl[:, :e_hg].astype(BF16)
        w_ssd = wl[:, e_hg:e_ssd].astype(BF16)
        w_dt = jnp.zeros((d, LANES), BF16).at[:, :2 * SSM_HEADS].set(wl[:, e_ssd:e_dt].astype(BF16))
        w_att = wl[:, e_dt:e_att].astype(BF16)
        w_gate = wl[:, e_att:].astype(BF16)
        w_br = w_branch[l].astype(BF16)
        w_o = w_out[l].astype(BF16)

        prm = jnp.zeros((SUBLANES, LANES), F32)
        prm = prm.at[0, :2 * SSM_HEADS].set(ssm_dt_bias[l].reshape(-1).astype(F32))
        prm = prm.at[1, :2 * SSM_HEADS].set(-jnp.exp(ssm_a_log[l].reshape(-1).astype(F32)))
        prm = prm.at[2, :2 * SSM_HEADS].set(ssm_d[l].reshape(-1).astype(F32))

        def mixer_inputs(xs, nb, t, shift, scale):
            h = norm_modulate(xs.reshape(nb, t, d), norm_mix_w[l], shift, scale).reshape(nb * t, d)
            p_hg = matmul(h, w_hg, F32).reshape(nb, t, e_hg)
            p_ssd = matmul(h, w_ssd, F32).reshape(nb, t, e_ssd - e_hg)
            p_dt = matmul(h, w_dt, F32).reshape(nb, t, LANES)
            p_att = matmul(h, w_att, F32)
            xbc = ssd_conv_silu(p_ssd, ssm_conv_w[l], ssm_conv_b[l])
            return h, p_hg, p_ssd, p_dt, p_att, xbc

        h_c, phg_c, pssd_c, pdt_c, patt_c, xbc_c = mixer_inputs(xc2, bsz, clen, mc[0], mc[1])
        h_l, phg_l, pssd_l, pdt_l, patt_l, xbc_l = mixer_inputs(x2, bsz, seq, ml[0], ml[1])

        ofc, s_f = hgrn2_scan(phg_c, lb_all[l, 0], zeros_hg, False)
        obc, s_b = hgrn2_scan(phg_c, lb_all[l, 1], zeros_hg, True)
        ofl, _ = hgrn2_scan(phg_l, lb_all[l, 0], s_f, False)
        obl, _ = hgrn2_scan(phg_l, lb_all[l, 1], s_b, True)
        ya_l = hgrn2_readout(ofl.reshape(m_lat, -1), obl.reshape(m_lat, -1), phg_l.reshape(m_lat, -1), hg_norm_w[l])

        yfc, h_f = ssd_scan(xbc_c, pdt_c, prm, zeros_ssd, False)
        ybc, h_b = ssd_scan(xbc_c, pdt_c, prm, zeros_ssd, True)
        yfl, _ = ssd_scan(xbc_l, pdt_l, prm, h_f, False)
        ybl, _ = ssd_scan(xbc_l, pdt_l, prm, h_b, True)
        yb_l = ssd_readout(yfl.reshape(m_lat, -1), ybl.reshape(m_lat, -1), pssd_l.reshape(m_lat, -1), ssm_norm_w[l])

        q_c, k_c, v_c = attention_prep(patt_c, attn_q_norm_w[l], attn_k_norm_w[l], cos, sin, clen, False)
        q_l, k_l, v_l = attention_prep(patt_l, attn_q_norm_w[l], attn_k_norm_w[l], cos, sin, seq, True)
        k_c3, v_c3 = k_c.reshape(bsz, clen, -1), v_c.reshape(bsz, clen, -1)
        k_all = jnp.concatenate([k_c3, k_l.reshape(bsz, seq, -1)], axis=1)
        v_all = jnp.concatenate([v_c3, v_l.reshape(bsz, seq, -1)], axis=1)
        yc_l = gqa_attention(q_l.reshape(bsz, seq, -1), k_all, v_all).reshape(m_lat, -1)

        gates_l = matmul(h_l, w_gate, BF16, act="sigmoid")
        merged_l = merge_branches(ya_l, yb_l, yc_l, gates_l, w_br)
        x2 = matmul_residual(merged_l, w_o, x2, ml[2])

        if need_ctx:
            ya_c = hgrn2_readout(ofc.reshape(m_ctx, -1), obc.reshape(m_ctx, -1), phg_c.reshape(m_ctx, -1), hg_norm_w[l])
            yb_c = ssd_readout(yfc.reshape(m_ctx, -1), ybc.reshape(m_ctx, -1), pssd_c.reshape(m_ctx, -1), ssm_norm_w[l])
            yc_c = gqa_attention(q_c.reshape(bsz, clen, -1), k_c3, v_c3).reshape(m_ctx, -1)
            gates_c = matmul(h_c, w_gate, BF16, act="sigmoid")
            merged_c = merge_branches(ya_c, yb_c, yc_c, gates_c, w_br)
            xc2 = matmul_residual(merged_c, w_o, xc2, mc[2])

        if l % 2 == 0:
            wg = ffn_w_gate[l // 2].astype(BF16)
            wu = ffn_w_up[l // 2].astype(BF16)
            wd = ffn_w_down[l // 2].astype(BF16)
            hf = norm_modulate(x2.reshape(bsz, seq, d), norm_ffn_w[l], ml[3], ml[4]).reshape(m_lat, d)
            x2 = swiglu_residual(hf, wg, wu, wd, x2, ml[5])
            if need_ctx:
                hfc = norm_modulate(xc2.reshape(bsz, clen, d), norm_ffn_w[l], mc[3], mc[4]).reshape(m_ctx, d)
                xc2 = swiglu_residual(hfc, wg, wu, wd, xc2, mc[5])
        else:
            wg = moe_w_gate[l // 2].astype(BF16)
            wu = moe_w_up[l // 2].astype(BF16)
            wd = moe_w_down[l // 2].astype(BF16)
            x2 = moe_swiglu_residual(x2.reshape(bsz, seq, d), norm_ffn_w[l], ml[3], ml[4], moe_router[l // 2],
                                     wg, wu, wd, ml[5])
            if need_ctx:
                xc2 = moe_swiglu_residual(xc2.reshape(bsz, clen, d), norm_ffn_w[l],
                                          jnp.broadcast_to(mc[3], (bsz, 1, d)), jnp.broadcast_to(mc[4], (bsz, 1, d)),
                                          moe_router[l // 2], wg, wu, wd, mc[5])

    return rms_norm_rows(x2, final_norm_w).reshape(bsz, seq, d)
```

```python
import functools
import math

import jax
import jax.numpy as jnp
from jax import lax
from jax.experimental import pallas as pl
from jax.experimental.pallas import tpu as pltpu

F32 = jnp.float32
BF16 = jnp.bfloat16

EPS = 1e-6
GRID_W = 64
HG_HEADS = 8
HG_DK = 128
HG_WIDTH = HG_HEADS * HG_DK
HG_BLOCK = 16
SSM_HEADS = 16
SSM_HEAD_DIM = 64
SSM_WIDTH = SSM_HEADS * SSM_HEAD_DIM
SSM_GROUPS = 4
SSM_STATE = 128
SSM_CONV = 5
SSM_CONV_CH = SSM_WIDTH + 2 * SSM_GROUPS * SSM_STATE
SSM_CHUNK = 128
ATT_HEADS = 8
ATT_KV_HEADS = 2
ATT_HEAD_DIM = 128
ATT_GROUP = ATT_HEADS // ATT_KV_HEADS
ATT_WIDTH = ATT_HEADS * ATT_HEAD_DIM
ATT_KV_WIDTH = ATT_KV_HEADS * ATT_HEAD_DIM
ATT_Q_SCALE = ATT_HEAD_DIM ** -0.5 * math.log2(math.e)
ROPE_THETA = 10000.0
N_BRANCHES = 3
N_EXPERTS = 8
LANES = 128
SUBLANES = 8
VMEM_LIMIT = 56 * 1024 * 1024


def _params(*sem):
    return pltpu.CompilerParams(dimension_semantics=sem, vmem_limit_bytes=VMEM_LIMIT)


def _tile(n, prefs):
    for t in prefs:
        if n % t == 0:
            return t
    return n


def _sigmoid(x):
    return 1.0 / (1.0 + jnp.exp(-x))


def _silu(x):
    return x / (1.0 + jnp.exp(-x))


def _dot(a, b):
    return jnp.dot(a, b, preferred_element_type=F32)


def _dot_nt(a, b):
    return lax.dot_general(a, b, (((1,), (1,)), ((), ())), preferred_element_type=F32)


def _dot_tn(a, b):
    return lax.dot_general(a, b, (((0,), (0,)), ((), ())), preferred_element_type=F32)


def _ada_kernel(c_ref, w_ref, b_ref, o_ref):
    s = _silu(c_ref[...])
    o_ref[...] = jnp.dot(s, w_ref[...], preferred_element_type=F32,
                         precision=lax.Precision.HIGHEST) + b_ref[...]


def ada_modulation(cc, w, b):
    rows, d = cc.shape
    n = w.shape[1]
    tn = _tile(n, (1024, 512, 256, 128))
    return pl.pallas_call(
        _ada_kernel,
        out_shape=jax.ShapeDtypeStruct((rows, n), F32),
        grid=(n // tn,),
        in_specs=[pl.BlockSpec((rows, d), lambda j: (0, 0)),
                  pl.BlockSpec((d, tn), lambda j: (0, j)),
                  pl.BlockSpec((1, tn), lambda j: (0, j))],
        out_specs=pl.BlockSpec((rows, tn), lambda j: (0, j)),
        compiler_params=_params("arbitrary"),
        name="ada_modulation",
    )(cc, w, b.reshape(1, n))


def _normmod_kernel(x_ref, w_ref, sh_ref, sc_ref, o_ref):
    x = x_ref[0]
    ms = jnp.mean(x * x, axis=-1, keepdims=True)
    y = x * lax.rsqrt(ms + EPS) * w_ref[...]
    o_ref[0] = (y * (1.0 + sc_ref[0]) + sh_ref[0]).astype(o_ref.dtype)


def norm_modulate(x, w, shift, scale, out_dtype=BF16):
    b, t, d = x.shape
    tm = _tile(t, (512, 256, 128))
    bm = shift.shape[0]
    mod_map = (lambda i, j: (i, 0, 0)) if bm == b else (lambda i, j: (0, 0, 0))
    return pl.pallas_call(
        _normmod_kernel,
        out_shape=jax.ShapeDtypeStruct((b, t, d), out_dtype),
        grid=(b, t // tm),
        in_specs=[pl.BlockSpec((1, tm, d), lambda i, j: (i, j, 0)),
                  pl.BlockSpec((1, d), lambda i, j: (0, 0)),
                  pl.BlockSpec((1, 1, d), mod_map),
                  pl.BlockSpec((1, 1, d), mod_map)],
        out_specs=pl.BlockSpec((1, tm, d), lambda i, j: (i, j, 0)),
        compiler_params=_params("arbitrary", "arbitrary"),
        name="norm_modulate",
    )(x, w.reshape(1, d), shift, scale)


def _rmsnorm_kernel(x_ref, w_ref, o_ref):
    x = x_ref[...]
    ms = jnp.mean(x * x, axis=-1, keepdims=True)
    o_ref[...] = x * lax.rsqrt(ms + EPS) * w_ref[...]


def rms_norm_rows(x, w):
    m, d = x.shape
    tm = _tile(m, (512, 256, 128))
    return pl.pallas_call(
        _rmsnorm_kernel,
        out_shape=jax.ShapeDtypeStruct((m, d), F32),
        grid=(m // tm,),
        in_specs=[pl.BlockSpec((tm, d), lambda i: (i, 0)),
                  pl.BlockSpec((1, d), lambda i: (0, 0))],
        out_specs=pl.BlockSpec((tm, d), lambda i: (i, 0)),
        compiler_params=_params("arbitrary"),
        name="final_rms_norm",
    )(x, w.reshape(1, d))


def _mm_kernel(a_ref, w_ref, o_ref, *, act):
    acc = _dot(a_ref[...], w_ref[...])
    if act == "sigmoid":
        acc = _sigmoid(acc)
    o_ref[...] = acc.astype(o_ref.dtype)


def matmul(a, w, out_dtype, act=None):
    m, k = a.shape
    n = w.shape[1]
    tm = _tile(m, (1024, 512, 256, 128))
    tn = _tile(n, (1024, 512, 256, 128))
    return pl.pallas_call(
        functools.partial(_mm_kernel, act=act),
        out_shape=jax.ShapeDtypeStruct((m, n), out_dtype),
        grid=(m // tm, n // tn),
        in_specs=[pl.BlockSpec((tm, k), lambda i, j: (i, 0)),
                  pl.BlockSpec((k, tn), lambda i, j: (0, j))],
        out_specs=pl.BlockSpec((tm, tn), lambda i, j: (i, j)),
        compiler_params=_params("arbitrary", "arbitrary"),
        name="matmul_" + (act or "plain"),
    )(a, w)


def _mm_res_kernel(a_ref, w_ref, r_ref, m_ref, o_ref):
    o_ref[...] = r_ref[...] + m_ref[0] * _dot(a_ref[...], w_ref[...])


def matmul_residual(a, w, res, mod):
    m, k = a.shape
    n = w.shape[1]
    rows_per_mod = m // mod.shape[0]
    tm = _tile(rows_per_mod, (1024, 512, 256, 128))
    tn = _tile(n, (1024, 512, 256, 128))
    return pl.pallas_call(
        _mm_res_kernel,
        out_shape=jax.ShapeDtypeStruct((m, n), F32),
        grid=(m // tm, n // tn),
        in_specs=[pl.BlockSpec((tm, k), lambda i, j: (i, 0)),
                  pl.BlockSpec((k, tn), lambda i, j: (0, j)),
                  pl.BlockSpec((tm, tn), lambda i, j: (i, j)),
                  pl.BlockSpec((1, 1, tn), lambda i, j: (i * tm // rows_per_mod, 0, j))],
        out_specs=pl.BlockSpec((tm, tn), lambda i, j: (i, j)),
        compiler_params=_params("arbitrary", "arbitrary"),
        name="matmul_residual",
    )(a, w, res, mod)


def _merge_kernel(ya_ref, yb_ref, yc_ref, g0_ref, g1_ref, g2_ref, wb_ref, o_ref):
    acc = g0_ref[...].astype(F32) * _dot(ya_ref[...], wb_ref[0])
    acc = acc + g1_ref[...].astype(F32) * _dot(yb_ref[...], wb_ref[1])
    acc = acc + g2_ref[...].astype(F32) * _dot(yc_ref[...], wb_ref[2])
    o_ref[...] = acc.astype(o_ref.dtype)


def merge_branches(ya, yb, yc, gates, wb):
    m, kb = ya.shape
    d = wb.shape[2]
    tm = _tile(m, (1024, 512, 256, 128))
    tn = _tile(d, (512, 256, 128))
    nj = d // tn
    y_spec = pl.BlockSpec((tm, kb), lambda i, j: (i, 0))

    def g_spec(nb):
        return pl.BlockSpec((tm, tn), lambda i, j: (i, nb * nj + j))

    return pl.pallas_call(
        _merge_kernel,
        out_shape=jax.ShapeDtypeStruct((m, d), BF16),
        grid=(m // tm, nj),
        in_specs=[y_spec, y_spec, y_spec, g_spec(0), g_spec(1), g_spec(2),
                  pl.BlockSpec((N_BRANCHES, kb, tn), lambda i, j: (0, 0, j))],
        out_specs=pl.BlockSpec((tm, tn), lambda i, j: (i, j)),
        compiler_params=_params("arbitrary", "arbitrary"),
        name="merge_branches",
    )(ya, yb, yc, gates, gates, gates, wb)


def _ffn_kernel(h_ref, wg_ref, wu_ref, wd_ref, x_ref, m_ref, o_ref, acc_ref):
    f = pl.program_id(1)

    @pl.when(f == 0)
    def _():
        acc_ref[...] = jnp.zeros_like(acc_ref)

    h = h_ref[...]
    a = _silu(_dot(h, wg_ref[...])) * _dot(h, wu_ref[...])
    acc_ref[...] += _dot(a.astype(BF16), wd_ref[...])

    @pl.when(f == pl.num_programs(1) - 1)
    def _():
        o_ref[...] = x_ref[...] + m_ref[0] * acc_ref[...]


def swiglu_residual(h, wg, wu, wd, x, mod):
    m, d = h.shape
    ff = wg.shape[1]
    rows_per_mod = m // mod.shape[0]
    tm = _tile(rows_per_mod, (512, 256, 128))
    tf = _tile(ff, (512, 256, 128))
    return pl.pallas_call(
        _ffn_kernel,
        out_shape=jax.ShapeDtypeStruct((m, d), F32),
        grid=(m // tm, ff // tf),
        in_specs=[pl.BlockSpec((tm, d), lambda i, f: (i, 0)),
                  pl.BlockSpec((d, tf), lambda i, f: (0, f)),
                  pl.BlockSpec((d, tf), lambda i, f: (0, f)),
                  pl.BlockSpec((tf, d), lambda i, f: (f, 0)),
                  pl.BlockSpec((tm, d), lambda i, f: (i, 0)),
                  pl.BlockSpec((1, 1, d), lambda i, f: (i * tm // rows_per_mod, 0, 0))],
        out_specs=pl.BlockSpec((tm, d), lambda i, f: (i, 0)),
        scratch_shapes=[pltpu.VMEM((tm, d), F32)],
        compiler_params=_params("arbitrary", "arbitrary"),
        name="swiglu_residual",
    )(h, wg, wu, wd, x, mod)


MOE_TILE = 512
ROUTE_E0, ROUTE_E1, ROUTE_W0, ROUTE_W1, ROUTE_P0, ROUTE_P1 = range(6)


def _router_kernel(x_ref, w_ref, sh_ref, sc_ref, rw_ref, h_ref, r_ref, cnt_ref, run_ref):
    @pl.when((pl.program_id(0) == 0) & (pl.program_id(1) == 0))
    def _():
        run_ref[...] = jnp.zeros_like(run_ref)

    x = x_ref[0]
    ms = jnp.mean(x * x, axis=-1, keepdims=True)
    h = x * lax.rsqrt(ms + EPS) * w_ref[...] * (1.0 + sc_ref[0]) + sh_ref[0]
    h_ref[0] = h
    logits = jnp.dot(h, rw_ref[...], preferred_element_type=F32, precision=lax.Precision.HIGHEST)
    lane = lax.broadcasted_iota(jnp.int32, logits.shape, 1)
    logits = jnp.where(lane < N_EXPERTS, logits, -jnp.inf)
    m1 = jnp.max(logits, axis=-1, keepdims=True)
    i1 = jnp.min(jnp.where(logits == m1, lane, LANES), axis=-1, keepdims=True)
    rest = jnp.where(lane == i1, -jnp.inf, logits)
    m2 = jnp.max(rest, axis=-1, keepdims=True)
    i2 = jnp.min(jnp.where(rest == m2, lane, LANES), axis=-1, keepdims=True)
    e2 = jnp.exp(m2 - m1)
    den = 1.0 + e2
    sel1 = lane == i1
    sel2 = lane == i2
    onehot = jnp.where(sel1 | sel2, 1.0, 0.0)
    tm = onehot.shape[0]
    tri = jnp.where(lax.broadcasted_iota(jnp.int32, (tm, tm), 0) > lax.broadcasted_iota(jnp.int32, (tm, tm), 1),
                    1.0, 0.0).astype(BF16)
    ranks = _dot(tri, onehot.astype(BF16)) + run_ref[...]
    p1 = jnp.sum(jnp.where(sel1, ranks, 0.0), axis=-1, keepdims=True)
    p2 = jnp.sum(jnp.where(sel2, ranks, 0.0), axis=-1, keepdims=True)
    run_ref[...] += jnp.sum(onehot, axis=0, keepdims=True)
    cnt_ref[...] = jnp.broadcast_to(run_ref[...], cnt_ref.shape)
    rec = jnp.where(lane == ROUTE_E0, i1.astype(F32), 0.0)
    rec = jnp.where(lane == ROUTE_E1, i2.astype(F32), rec)
    rec = jnp.where(lane == ROUTE_W0, 1.0 / den, rec)
    rec = jnp.where(lane == ROUTE_W1, e2 / den, rec)
    rec = jnp.where(lane == ROUTE_P0, p1, rec)
    rec = jnp.where(lane == ROUTE_P1, p2, rec)
    r_ref[0] = rec


def route_top2(x, w, shift, scale, router_w):
    b, t, d = x.shape
    tm = _tile(t, (512, 256, 128))
    rw = jnp.zeros((d, LANES), F32).at[:, :N_EXPERTS].set(router_w)
    return pl.pallas_call(
        _router_kernel,
        out_shape=(jax.ShapeDtypeStruct((b, t, d), F32), jax.ShapeDtypeStruct((b, t, LANES), F32),
                   jax.ShapeDtypeStruct((SUBLANES, LANES), F32)),
        grid=(b, t // tm),
        in_specs=[pl.BlockSpec((1, tm, d), lambda i, j: (i, j, 0)),
                  pl.BlockSpec((1, d), lambda i, j: (0, 0)),
                  pl.BlockSpec((1, 1, d), lambda i, j: (i, 0, 0)),
                  pl.BlockSpec((1, 1, d), lambda i, j: (i, 0, 0)),
                  pl.BlockSpec((d, LANES), lambda i, j: (0, 0))],
        out_specs=(pl.BlockSpec((1, tm, d), lambda i, j: (i, j, 0)),
                   pl.BlockSpec((1, tm, LANES), lambda i, j: (i, j, 0)),
                   pl.BlockSpec((SUBLANES, LANES), lambda i, j: (0, 0))),
        scratch_shapes=[pltpu.VMEM((1, LANES), F32)],
        compiler_params=_params("arbitrary", "arbitrary"),
        name="moe_router",
    )(x, w.reshape(1, d), shift, scale, rw)


def _row_copy(src, src_row, dst, dst_row, sem):
    return pltpu.make_async_copy(src.at[pl.ds(src_row, 1)], dst.at[pl.ds(dst_row, 1)], sem)


def _dispatch_kernel(d0_ref, d1_ref, h_ref, init_hbm, hs_hbm, sem, *, tm):
    del init_hbm
    base = pl.program_id(0) * tm

    def issue(r, carry):
        t = base + r
        _row_copy(h_ref, r, hs_hbm, d0_ref[t], sem).start()
        _row_copy(h_ref, r, hs_hbm, d1_ref[t], sem).start()
        return carry

    def drain(r, carry):
        _row_copy(h_ref, 0, hs_hbm, 0, sem).wait()
        _row_copy(h_ref, 0, hs_hbm, 0, sem).wait()
        return carry

    lax.fori_loop(0, tm, issue, 0)
    lax.fori_loop(0, tm, drain, 0)


def moe_dispatch(h, dest0, dest1, n_rows):
    m, d = h.shape
    tm = _tile(m, (256, 128))
    any_spec = pl.BlockSpec(memory_space=pl.ANY)
    return pl.pallas_call(
        functools.partial(_dispatch_kernel, tm=tm),
        out_shape=jax.ShapeDtypeStruct((n_rows, d), F32),
        grid_spec=pltpu.PrefetchScalarGridSpec(
            num_scalar_prefetch=2, grid=(m // tm,),
            in_specs=[pl.BlockSpec((tm, d), lambda i, d0, d1: (i, 0)), any_spec], out_specs=any_spec,
            scratch_shapes=[pltpu.SemaphoreType.DMA]),
        input_output_aliases={3: 0},
        compiler_params=_params("arbitrary"),
        name="moe_dispatch",
    )(dest0, dest1, h, jnp.zeros((n_rows, d), F32))


def _grouped_ffn_kernel(te_ref, nu_ref, hs_ref, wg_ref, wu_ref, wd_ref, o_ref, acc_ref, hb_ref):
    del te_ref
    i = pl.program_id(0)
    f = pl.program_id(1)

    @pl.when(f == 0)
    def _():
        acc_ref[...] = jnp.zeros_like(acc_ref)
        hb_ref[...] = hs_ref[...].astype(BF16)

    @pl.when(i < nu_ref[0])
    def _():
        h = hb_ref[...]
        a = _silu(_dot(h, wg_ref[0])) * _dot(h, wu_ref[0])
        acc_ref[...] += _dot(a.astype(BF16), wd_ref[0])

    @pl.when(f == pl.num_programs(1) - 1)
    def _():
        o_ref[...] = acc_ref[...]


def moe_grouped_swiglu(hs, tile_expert, n_used, wg, wu, wd):
    p, d = hs.shape
    ff = wg.shape[2]
    tm = MOE_TILE
    tf = _tile(ff, (512, 256, 128))

    def fblock(i, f, nu):
        return jnp.where(i < nu[0], f, 0)

    return pl.pallas_call(
        _grouped_ffn_kernel,
        out_shape=jax.ShapeDtypeStruct((p, d), F32),
        grid_spec=pltpu.PrefetchScalarGridSpec(
            num_scalar_prefetch=2, grid=(p // tm, ff // tf),
            in_specs=[pl.BlockSpec((tm, d), lambda i, f, te, nu: (i, 0)),
                      pl.BlockSpec((1, d, tf), lambda i, f, te, nu: (te[i], 0, fblock(i, f, nu))),
                      pl.BlockSpec((1, d, tf), lambda i, f, te, nu: (te[i], 0, fblock(i, f, nu))),
                      pl.BlockSpec((1, tf, d), lambda i, f, te, nu: (te[i], fblock(i, f, nu), 0))],
            out_specs=pl.BlockSpec((tm, d), lambda i, f, te, nu: (i, 0)),
            scratch_shapes=[pltpu.VMEM((tm, d), F32), pltpu.VMEM((tm, d), BF16)]),
        compiler_params=_params("arbitrary", "arbitrary"),
        name="moe_grouped_swiglu",
    )(tile_expert, n_used, hs, wg, wu, wd)


def _combine_kernel(d0_ref, d1_ref, ys_hbm, r_ref, x_ref, m_ref, o_ref, buf_ref, sem, *, tm):
    base = pl.program_id(0) * tm

    def issue(r, carry):
        t = base + r
        _row_copy(ys_hbm, d0_ref[t], buf_ref.at[0], r, sem).start()
        _row_copy(ys_hbm, d1_ref[t], buf_ref.at[1], r, sem).start()
        return carry

    def drain(r, carry):
        _row_copy(ys_hbm, 0, buf_ref.at[0], 0, sem).wait()
        _row_copy(ys_hbm, 0, buf_ref.at[1], 0, sem).wait()
        return carry

    lax.fori_loop(0, tm, issue, 0)
    lax.fori_loop(0, tm, drain, 0)
    rec = r_ref[...]
    lane = lax.broadcasted_iota(jnp.int32, rec.shape, 1)
    w0 = jnp.sum(jnp.where(lane == ROUTE_W0, rec, 0.0), axis=-1, keepdims=True)
    w1 = jnp.sum(jnp.where(lane == ROUTE_W1, rec, 0.0), axis=-1, keepdims=True)
    o_ref[...] = x_ref[...] + m_ref[0] * (w0 * buf_ref[0] + w1 * buf_ref[1])


def moe_combine(ys, dest0, dest1, route, x, mod):
    m, d = x.shape
    rows_per_mod = m // mod.shape[0]
    tm = _tile(rows_per_mod, (256, 128))
    return pl.pallas_call(
        functools.partial(_combine_kernel, tm=tm),
        out_shape=jax.ShapeDtypeStruct((m, d), F32),
        grid_spec=pltpu.PrefetchScalarGridSpec(
            num_scalar_prefetch=2, grid=(m // tm,),
            in_specs=[pl.BlockSpec(memory_space=pl.ANY),
                      pl.BlockSpec((tm, LANES), lambda i, d0, d1: (i, 0)),
                      pl.BlockSpec((tm, d), lambda i, d0, d1: (i, 0)),
                      pl.BlockSpec((1, 1, d), lambda i, d0, d1: (i * tm // rows_per_mod, 0, 0))],
            out_specs=pl.BlockSpec((tm, d), lambda i, d0, d1: (i, 0)),
            scratch_shapes=[pltpu.VMEM((2, tm, d), F32), pltpu.SemaphoreType.DMA]),
        compiler_params=_params("arbitrary"),
        name="moe_combine",
    )(dest0, dest1, ys, route, x, mod)


def moe_swiglu_residual(x3, norm_w, shift, scale, router_w, wg, wu, wd, mod):
    b, t, d = x3.shape
    m = b * t
    h, route, counts = route_top2(x3, norm_w, shift, scale, router_w)
    route = route.reshape(m, LANES)
    cnt = counts[0, :N_EXPERTS].astype(jnp.int32)
    padded = (cnt + MOE_TILE - 1) // MOE_TILE * MOE_TILE
    ends = jnp.cumsum(padded)
    starts = ends - padded
    e0 = route[:, ROUTE_E0].astype(jnp.int32)
    e1 = route[:, ROUTE_E1].astype(jnp.int32)
    dest0 = starts[e0] + route[:, ROUTE_P0].astype(jnp.int32)
    dest1 = starts[e1] + route[:, ROUTE_P1].astype(jnp.int32)
    n_rows = 2 * m + N_EXPERTS * MOE_TILE
    n_tiles = n_rows // MOE_TILE
    n_used = ends[-1] // MOE_TILE
    tile_ids = jnp.minimum(jnp.arange(n_tiles, dtype=jnp.int32), n_used - 1)
    tile_expert = jnp.sum((tile_ids[:, None] >= (ends // MOE_TILE)[None, :]).astype(jnp.int32), axis=1)
    tile_expert = jnp.minimum(tile_expert, N_EXPERTS - 1).astype(jnp.int32)
    hs = moe_dispatch(h.reshape(m, d), dest0, dest1, n_rows)
    ys = moe_grouped_swiglu(hs, tile_expert, n_used.reshape(1).astype(jnp.int32), wg, wu, wd)
    return moe_combine(ys, dest0, dest1, route, x3.reshape(m, d), mod)


def _hgrn2_kernel(q_ref, f_ref, v_ref, lb_ref, s0_ref, o_ref, sout_ref, st_ref, c_ref, k_ref, wblk_ref,
                  *, reverse, rows):
    j = pl.program_id(1)

    @pl.when(j == 0)
    def _():
        st_ref[...] = s0_ref[0]

    lb = lb_ref[...]
    fr = f_ref[0]
    log_sig = jnp.minimum(fr, 0.0) - jnp.log(1.0 + jnp.exp(-jnp.abs(fr)))
    a = jnp.log(lb)
    b = jnp.log(1.0 - lb) + log_sig
    log_f = jnp.maximum(a, b) + jnp.log(1.0 + jnp.exp(-jnp.abs(a - b)))
    k_ref[...] = (1.0 - lb) / (1.0 + jnp.exp(fr))
    rib = lax.broadcasted_iota(jnp.int32, fr.shape, 0) & (HG_BLOCK - 1)
    c = log_f
    sh = 1
    while sh < HG_BLOCK:
        if reverse:
            c = c + jnp.where(rib < HG_BLOCK - sh, pltpu.roll(c, rows - sh, axis=0), 0.0)
        else:
            c = c + jnp.where(rib >= sh, pltpu.roll(c, sh, axis=0), 0.0)
        sh *= 2
    c_ref[...] = c

    trow = lax.broadcasted_iota(jnp.int32, (HG_BLOCK, HG_DK), 0)
    blocks = list(range(rows // HG_BLOCK))
    if reverse:
        blocks = blocks[::-1]
    pw = 2 * HG_DK

    def head_pair(p):
        cols = slice(p * pw, (p + 1) * pw)
        wblk = wblk_ref.at[p]
        s_a = st_ref[2 * p]
        s_b = st_ref[2 * p + 1]
        wblk[0:HG_DK, 0:HG_DK] = s_a.astype(BF16)
        wblk[HG_DK:pw, HG_DK:pw] = s_b.astype(BF16)
        for blk in blocks:
            r0 = blk * HG_BLOCK
            qb = q_ref[0, r0:r0 + HG_BLOCK, cols] * (HG_DK ** -0.5)
            vb = v_ref[0, r0:r0 + HG_BLOCK, cols]
            kb = k_ref[r0:r0 + HG_BLOCK, cols]
            cb = c_ref[r0:r0 + HG_BLOCK, cols]
            o = _dot_nt((qb * jnp.exp(cb)).astype(BF16), wblk[...])
            diag = []
            for hk in range(2):
                sl = slice(hk * HG_DK, (hk + 1) * HG_DK)
                qh, kh, vh, ch = qb[:, sl], kb[:, sl], vb[:, sl], cb[:, sl]
                oh = jnp.zeros((HG_BLOCK, HG_DK), F32)
                for jj in range(HG_BLOCK):
                    valid = (trow <= jj) if reverse else (trow >= jj)
                    e = jnp.exp(jnp.where(valid, ch - ch[jj:jj + 1], -jnp.inf)) * (qh * kh[jj:jj + 1])
                    oh = oh + jnp.sum(e, axis=-1, keepdims=True) * vh[jj:jj + 1]
                diag.append(oh)
            o_ref[0, r0:r0 + HG_BLOCK, cols] = o + jnp.concatenate(diag, axis=1)
            c_tot = cb[0:1] if reverse else cb[HG_BLOCK - 1:HG_BLOCK]
            kt = (kb * jnp.exp(c_tot - cb)).astype(BF16)
            upd = _dot_tn(vb.astype(BF16), kt)
            grow = jnp.exp(c_tot)
            s_a = s_a * grow[:, :HG_DK] + upd[:HG_DK, :HG_DK]
            s_b = s_b * grow[:, HG_DK:] + upd[HG_DK:, HG_DK:]
            wblk[0:HG_DK, 0:HG_DK] = s_a.astype(BF16)
            wblk[HG_DK:pw, HG_DK:pw] = s_b.astype(BF16)
        st_ref[2 * p] = s_a
        st_ref[2 * p + 1] = s_b

    wblk_ref[...] = jnp.zeros_like(wblk_ref)
    for p in range(HG_HEADS // 2):
        head_pair(p)

    @pl.when(j == pl.num_programs(1) - 1)
    def _():
        sout_ref[0] = st_ref[...]


def hgrn2_scan(p_hg, lb, s0, reverse):
    b, t, _ = p_hg.shape
    w = HG_WIDTH
    rows = _tile(t, (64, 32, 16))
    nt = t // rows
    tmap = (lambda j: nt - 1 - j) if reverse else (lambda j: j)
    fcol = 2 if reverse else 1
    return pl.pallas_call(
        functools.partial(_hgrn2_kernel, reverse=reverse, rows=rows),
        out_shape=(jax.ShapeDtypeStruct((b, t, w), F32),
                   jax.ShapeDtypeStruct(s0.shape, F32)),
        grid=(b, nt),
        in_specs=[pl.BlockSpec((1, rows, w), lambda i, j: (i, tmap(j), 0)),
                  pl.BlockSpec((1, rows, w), lambda i, j: (i, tmap(j), fcol)),
                  pl.BlockSpec((1, rows, w), lambda i, j: (i, tmap(j), 3)),
                  pl.BlockSpec((1, w), lambda i, j: (0, 0)),
                  pl.BlockSpec((1, HG_HEADS, HG_DK, HG_DK), lambda i, j: (i, 0, 0, 0))],
        out_specs=(pl.BlockSpec((1, rows, w), lambda i, j: (i, tmap(j), 0)),
                   pl.BlockSpec((1, HG_HEADS, HG_DK, HG_DK), lambda i, j: (i, 0, 0, 0))),
        scratch_shapes=[pltpu.VMEM((HG_HEADS, HG_DK, HG_DK), F32),
                        pltpu.VMEM((rows, w), F32),
                        pltpu.VMEM((rows, w), F32),
                        pltpu.VMEM((HG_HEADS // 2, 2 * HG_DK, 2 * HG_DK), BF16)],
        compiler_params=_params("arbitrary", "arbitrary"),
        name="hgrn2_scan_bwd" if reverse else "hgrn2_scan_fwd",
    )(p_hg, p_hg, p_hg, lb.reshape(1, w), s0)


def _hg_readout_kernel(of_ref, ob_ref, g_ref, w_ref, y_ref):
    for h in range(HG_HEADS):
        sl = slice(h * HG_DK, (h + 1) * HG_DK)
        o = of_ref[:, sl] + ob_ref[:, sl]
        ms = jnp.mean(o * o, axis=-1, keepdims=True)
        y = o * lax.rsqrt(ms + EPS) * w_ref[:, sl]
        y_ref[:, sl] = (y * _silu(g_ref[:, sl])).astype(y_ref.dtype)


def hgrn2_readout(o_f, o_b, p_hg, norm_w):
    m, w = o_f.shape
    tm = _tile(m, (512, 256, 128))
    spec = pl.BlockSpec((tm, w), lambda i: (i, 0))
    return pl.pallas_call(
        _hg_readout_kernel,
        out_shape=jax.ShapeDtypeStruct((m, w), BF16),
        grid=(m // tm,),
        in_specs=[spec, spec, pl.BlockSpec((tm, w), lambda i: (i, 4)),
                  pl.BlockSpec((1, w), lambda i: (0, 0))],
        out_specs=spec,
        compiler_params=_params("arbitrary"),
        name="hgrn2_readout",
    )(o_f, o_b, p_hg, norm_w.reshape(1, w))


def _conv_kernel(prev_ref, cur_ref, next_ref, w_ref, b_ref, o_ref, *, tm):
    i = pl.program_id(1)
    prev = jnp.where(i > 0, prev_ref[0], 0.0)
    nxt = jnp.where(i < pl.num_programs(1) - 1, next_ref[0], 0.0)
    xe = jnp.concatenate([prev, cur_ref[0], nxt], axis=0)
    ext = tm + 2 * SUBLANES
    acc = jnp.zeros((tm, xe.shape[1]), F32) + b_ref[...]
    for k in range(SSM_CONV):
        sh = (SSM_CONV // 2 - k) % ext
        xs = xe if sh == 0 else pltpu.roll(xe, sh, axis=0)
        acc = acc + w_ref[k:k + 1, :] * xs[SUBLANES:SUBLANES + tm]
    o_ref[0] = _silu(acc)


def ssd_conv_silu(p_ssd, conv_w, conv_b):
    b, t, _ = p_ssd.shape
    ch = SSM_CONV_CH
    tm = _tile(t, (512, 256, 128))
    tc = 1024
    c0 = SSM_WIDTH // tc
    hb = tm // SUBLANES
    nh = t // SUBLANES
    return pl.pallas_call(
        functools.partial(_conv_kernel, tm=tm),
        out_shape=jax.ShapeDtypeStruct((b, t, ch), F32),
        grid=(b, t // tm, ch // tc),
        in_specs=[pl.BlockSpec((1, SUBLANES, tc), lambda i, j, cj: (i, jnp.maximum(j * hb - 1, 0), c0 + cj)),
                  pl.BlockSpec((1, tm, tc), lambda i, j, cj: (i, j, c0 + cj)),
                  pl.BlockSpec((1, SUBLANES, tc), lambda i, j, cj: (i, jnp.minimum((j + 1) * hb, nh - 1), c0 + cj)),
                  pl.BlockSpec((SUBLANES, tc), lambda i, j, cj: (0, cj)),
                  pl.BlockSpec((1, tc), lambda i, j, cj: (0, cj))],
        out_specs=pl.BlockSpec((1, tm, tc), lambda i, j, cj: (i, j, cj)),
        compiler_params=_params("arbitrary", "arbitrary", "arbitrary"),
        name="ssd_conv_silu",
    )(p_ssd, p_ssd, p_ssd,
      jnp.zeros((SUBLANES, ch), F32).at[:SSM_CONV].set(conv_w), conv_b.reshape(1, ch))


def _split3(x):
    hi = x.astype(BF16)
    r1 = x - hi.astype(F32)
    mid = r1.astype(BF16)
    lo = (r1 - mid.astype(F32)).astype(BF16)
    return hi, mid, lo


def _pick_lanes(parts, sel):
    return _dot(parts[0], sel) + _dot(parts[1], sel) + _dot(parts[2], sel)


def _ssd_kernel(x_ref, b_ref, c_ref, dt_ref, prm_ref, selh_ref, selp_ref, h0_ref, y_ref, hout_ref, hst_ref,
                *, reverse):
    j = pl.program_id(1)
    cl = SSM_CHUNK
    hd = SSM_HEAD_DIM

    @pl.when(j == 0)
    def _():
        hst_ref[...] = h0_ref[0]

    prm = prm_ref[...]
    dtr = dt_ref[0] + prm[0:1]
    dt = jnp.maximum(dtr, 0.0) + jnp.log(1.0 + jnp.exp(-jnp.abs(dtr)))
    a = dt * prm[1:2]
    row = lax.broadcasted_iota(jnp.int32, a.shape, 0)
    ac = a
    sh = 1
    while sh < cl:
        if reverse:
            ac = ac + jnp.where(row < cl - sh, pltpu.roll(ac, cl - sh, axis=0), 0.0)
        else:
            ac = ac + jnp.where(row >= sh, pltpu.roll(ac, sh, axis=0), 0.0)
        sh *= 2
    ac_t = ac.T
    ac_parts = _split3(ac)
    ac_head = _pick_lanes(ac_parts, selh_ref[...])
    ac_x = _pick_lanes(ac_parts, selp_ref[...])
    dt_x = _pick_lanes(_split3(dt), selp_ref[...])
    d_x = _pick_lanes(_split3(prm), selp_ref[...])[2:3]
    tot_x = ac_x[0:1] if reverse else ac_x[cl - 1:cl]
    xs = x_ref[0]
    xdt = xs * dt_x
    y_skip = xs * d_x
    exp_ac = jnp.exp(ac_x)
    xdec = (xdt * jnp.exp(tot_x - ac_x)).astype(BF16)
    exp_tot = jnp.exp(tot_x)
    xdt = xdt.astype(BF16)

    ti = lax.broadcasted_iota(jnp.int32, (cl, cl), 0)
    si = lax.broadcasted_iota(jnp.int32, (cl, cl), 1)
    mask = (ti <= si) if reverse else (ti >= si)
    first_head = si < hd
    zero = jnp.zeros((), BF16)
    lane0 = SSM_HEADS if reverse else 0
    pairs_per_group = SSM_HEADS // SSM_GROUPS // 2
    for g in range(SSM_GROUPS):
        bg = b_ref[0, :, g * SSM_STATE:(g + 1) * SSM_STATE].astype(BF16)
        cg = c_ref[0, :, g * SSM_STATE:(g + 1) * SSM_STATE].astype(BF16)
        cb = _dot_nt(cg, bg)
        for r in range(pairs_per_group):
            p = g * pairs_per_group + r
            cols = slice(p * 2 * hd, (p + 1) * 2 * hd)
            xp = xdt[:, cols]
            y = None
            for k, keep in ((0, first_head), (1, ~first_head)):
                h = 2 * p + k
                arow = ac_t[lane0 + h:lane0 + h + 1, :]
                acol = ac_head[:, h * LANES:(h + 1) * LANES]
                w = cb * jnp.exp(jnp.where(mask, acol - arow, -jnp.inf))
                yk = _dot(w.astype(BF16), jnp.where(keep, xp, zero))
                y = yk if y is None else y + yk
            hp = hst_ref[p]
            y = y + exp_ac[:, cols] * _dot_nt(cg, hp.astype(BF16)) + y_skip[:, cols]
            grow = jnp.where(ti < hd, exp_tot[:, p * 2 * hd:p * 2 * hd + 1], exp_tot[:, (p + 1) * 2 * hd - 1:(p + 1) * 2 * hd])
            hst_ref[p] = hp * grow + _dot_tn(xdec[:, cols], bg)
            y_ref[0, :, cols] = y

    @pl.when(j == pl.num_programs(1) - 1)
    def _():
        hout_ref[0] = hst_ref[...]


def ssd_scan(xbc, dt_raw, prm, h0, reverse):
    b, t, _ = xbc.shape
    cl = SSM_CHUNK
    nt = t // cl
    gw = SSM_GROUPS * SSM_STATE
    tmap = (lambda j: nt - 1 - j) if reverse else (lambda j: j)
    lane0 = SSM_HEADS if reverse else 0
    src = jnp.arange(LANES, dtype=jnp.int32)[:, None]
    sel_head = (src == lane0 + jnp.arange(SSM_HEADS * LANES, dtype=jnp.int32)[None, :] // LANES).astype(BF16)
    sel_x = (src == lane0 + jnp.arange(SSM_WIDTH, dtype=jnp.int32)[None, :] // SSM_HEAD_DIM).astype(BF16)
    st_spec = pl.BlockSpec((1,) + h0.shape[1:], lambda i, j: (i, 0, 0, 0))
    return pl.pallas_call(
        functools.partial(_ssd_kernel, reverse=reverse),
        out_shape=(jax.ShapeDtypeStruct((b, t, SSM_WIDTH), F32),
                   jax.ShapeDtypeStruct(h0.shape, F32)),
        grid=(b, nt),
        in_specs=[pl.BlockSpec((1, cl, SSM_WIDTH), lambda i, j: (i, tmap(j), 0)),
                  pl.BlockSpec((1, cl, gw), lambda i, j: (i, tmap(j), SSM_WIDTH // gw)),
                  pl.BlockSpec((1, cl, gw), lambda i, j: (i, tmap(j), SSM_WIDTH // gw + 1)),
                  pl.BlockSpec((1, cl, LANES), lambda i, j: (i, tmap(j), 0)),
                  pl.BlockSpec((SUBLANES, LANES), lambda i, j: (0, 0)),
                  pl.BlockSpec(sel_head.shape, lambda i, j: (0, 0)),
                  pl.BlockSpec(sel_x.shape, lambda i, j: (0, 0)),
                  st_spec],
        out_specs=(pl.BlockSpec((1, cl, SSM_WIDTH), lambda i, j: (i, tmap(j), 0)), st_spec),
        scratch_shapes=[pltpu.VMEM(h0.shape[1:], F32)],
        compiler_params=_params("arbitrary", "arbitrary"),
        name="ssd_scan_bwd" if reverse else "ssd_scan_fwd",
    )(xbc, xbc, xbc, dt_raw, prm, sel_head, sel_x, h0)


def _ssd_readout_kernel(yf_ref, yb_ref, z_ref, w_ref, o_ref):
    gw = SSM_WIDTH // SSM_GROUPS
    for g in range(SSM_GROUPS):
        sl = slice(g * gw, (g + 1) * gw)
        y = (yf_ref[:, sl] + yb_ref[:, sl]) * _silu(z_ref[:, sl])
        ms = jnp.mean(y * y, axis=-1, keepdims=True)
        o_ref[:, sl] = (y * lax.rsqrt(ms + EPS) * w_ref[:, sl]).astype(o_ref.dtype)


def ssd_readout(y_f, y_b, p_ssd, norm_w):
    m, w = y_f.shape
    tm = _tile(m, (512, 256, 128))
    spec = pl.BlockSpec((tm, w), lambda i: (i, 0))
    return pl.pallas_call(
        _ssd_readout_kernel,
        out_shape=jax.ShapeDtypeStruct((m, w), BF16),
        grid=(m // tm,),
        in_specs=[spec, spec, spec, pl.BlockSpec((1, w), lambda i: (0, 0))],
        out_specs=spec,
        compiler_params=_params("arbitrary"),
        name="ssd_readout",
    )(y_f, y_b, p_ssd, norm_w.reshape(1, w))


def _attprep_kernel(p_ref, qw_ref, kw_ref, cos_ref, sin_ref, q_ref, k_ref, v_ref, *, rope):
    lane = lax.broadcasted_iota(jnp.int32, (p_ref.shape[0], ATT_HEAD_DIM), 1)
    first = (lane % (ATT_HEAD_DIM // 2)) < (ATT_HEAD_DIM // 4)
    for h in range(ATT_HEADS + ATT_KV_HEADS):
        xh = p_ref[:, h * ATT_HEAD_DIM:(h + 1) * ATT_HEAD_DIM]
        w = qw_ref[...] if h < ATT_HEADS else kw_ref[...]
        ms = jnp.mean(xh * xh, axis=-1, keepdims=True)
        y = xh * lax.rsqrt(ms + EPS) * w
        if rope:
            partner = jnp.where(first, pltpu.roll(y, 3 * ATT_HEAD_DIM // 4, axis=1),
                                pltpu.roll(y, ATT_HEAD_DIM // 4, axis=1))
            y = y * cos_ref[...] + partner * sin_ref[...]
        if h < ATT_HEADS:
            q_ref[:, h * ATT_HEAD_DIM:(h + 1) * ATT_HEAD_DIM] = (y * ATT_Q_SCALE).astype(q_ref.dtype)
        else:
            hk = h - ATT_HEADS
            k_ref[:, hk * ATT_HEAD_DIM:(hk + 1) * ATT_HEAD_DIM] = y.astype(k_ref.dtype)
            v0 = ATT_WIDTH + ATT_KV_WIDTH + hk * ATT_HEAD_DIM
            v_ref[:, 2 * hk * ATT_HEAD_DIM:(2 * hk + 1) * ATT_HEAD_DIM] = p_ref[:, v0:v0 + ATT_HEAD_DIM].astype(v_ref.dtype)
            v_ref[:, (2 * hk + 1) * ATT_HEAD_DIM:(2 * hk + 2) * ATT_HEAD_DIM] = jnp.ones(
                (p_ref.shape[0], ATT_HEAD_DIM), v_ref.dtype)


def attention_prep(p_att, q_norm_w, k_norm_w, cos, sin, seq, rope):
    m, wtot = p_att.shape
    tm = _tile(seq, (512, 256, 128))
    npos = seq // tm
    tab = pl.BlockSpec((tm, ATT_HEAD_DIM), (lambda i: (i % npos, 0)) if rope else (lambda i: (0, 0)))
    wspec = pl.BlockSpec((1, ATT_HEAD_DIM), lambda i: (0, 0))
    return pl.pallas_call(
        functools.partial(_attprep_kernel, rope=rope),
        out_shape=(jax.ShapeDtypeStruct((m, ATT_WIDTH), BF16),
                   jax.ShapeDtypeStruct((m, ATT_KV_WIDTH), BF16),
                   jax.ShapeDtypeStruct((m, 2 * ATT_KV_WIDTH), BF16)),
        grid=(m // tm,),
        in_specs=[pl.BlockSpec((tm, wtot), lambda i: (i, 0)), wspec, wspec, tab, tab],
        out_specs=(pl.BlockSpec((tm, ATT_WIDTH), lambda i: (i, 0)),
                   pl.BlockSpec((tm, ATT_KV_WIDTH), lambda i: (i, 0)),
                   pl.BlockSpec((tm, 2 * ATT_KV_WIDTH), lambda i: (i, 0))),
        compiler_params=_params("arbitrary"),
        name="attention_prep_rope" if rope else "attention_prep",
    )(p_att, q_norm_w.reshape(1, -1), k_norm_w.reshape(1, -1), cos, sin)


def _attn_kernel(q_ref, k_ref, v_ref, o_ref, *, tk, nk):
    tq = q_ref.shape[1]
    qs = [q_ref[0, :, g * ATT_HEAD_DIM:(g + 1) * ATT_HEAD_DIM] for g in range(ATT_GROUP)]

    def body(c, carry):
        off = pl.multiple_of(c * tk, tk)
        kc = k_ref[0, pl.ds(off, tk), :]
        vc = v_ref[0, pl.ds(off, tk), :]
        out = []
        for g in range(ATT_GROUP):
            m, acc = carry[g]
            s = _dot_nt(qs[g], kc)
            mn = jnp.maximum(m, jnp.max(s, axis=-1, keepdims=True))
            p = jnp.exp2(s - mn)
            out.append((mn, jnp.exp2(m - mn) * acc + _dot(p.astype(BF16), vc)))
        return tuple(out)

    init = tuple((jnp.full((tq, 1), -jnp.inf, F32), jnp.zeros((tq, 2 * ATT_HEAD_DIM), F32))
                 for _ in range(ATT_GROUP))
    res = lax.fori_loop(0, nk, body, init)
    for g in range(ATT_GROUP):
        acc = res[g][1]
        o_ref[0, :, g * ATT_HEAD_DIM:(g + 1) * ATT_HEAD_DIM] = (
            acc[:, :ATT_HEAD_DIM] / acc[:, ATT_HEAD_DIM:]).astype(o_ref.dtype)


def gqa_attention(q, k, v):
    b, t, _ = q.shape
    s = k.shape[1]
    tq = _tile(t, (512, 256, 128))
    tk = s if s <= 1024 else _tile(s, (1408, 768, 512, 384, 256, 128))
    gwid = ATT_GROUP * ATT_HEAD_DIM
    return pl.pallas_call(
        functools.partial(_attn_kernel, tk=tk, nk=s // tk),
        out_shape=jax.ShapeDtypeStruct(q.shape, BF16),
        grid=(b, ATT_KV_HEADS, t // tq),
        in_specs=[pl.BlockSpec((1, tq, gwid), lambda i, h, j: (i, j, h)),
                  pl.BlockSpec((1, s, ATT_HEAD_DIM), lambda i, h, j: (i, 0, h)),
                  pl.BlockSpec((1, s, 2 * ATT_HEAD_DIM), lambda i, h, j: (i, 0, h))],
        out_specs=pl.BlockSpec((1, tq, gwid), lambda i, h, j: (i, j, h)),
        compiler_params=_params("arbitrary", "arbitrary", "arbitrary"),
        name="gqa_attention",
    )(q, k, v)


def _rope_tables(n_tokens):
    rows = n_tokens // GRID_W
    row = jnp.repeat(jnp.arange(rows, dtype=F32), GRID_W)
    col = jnp.tile(jnp.arange(GRID_W, dtype=F32), rows)
    axis_dim = ATT_HEAD_DIM // 2
    inv = ROPE_THETA ** (-jnp.arange(0, axis_dim, 2, dtype=F32) / axis_dim)
    ar, ac = row[:, None] * inv, col[:, None] * inv
    cos = jnp.concatenate([jnp.cos(ar), jnp.cos(ar), jnp.cos(ac), jnp.cos(ac)], axis=-1)
    sin = jnp.concatenate([-jnp.sin(ar), jnp.sin(ar), -jnp.sin(ac), jnp.sin(ac)], axis=-1)
    return cos, sin


def kernel(x, c, ctx, c_ctx, ada_w, ada_b, norm_mix_w, norm_ffn_w, w_in, hg_lb_logits, hg_norm_w, ssm_conv_w, ssm_conv_b, ssm_dt_bias, ssm_a_log, ssm_d, ssm_norm_w, attn_q_norm_w, attn_k_norm_w, w_branch, w_out, ffn_w_gate, ffn_w_up, ffn_w_down, moe_router, moe_w_gate, moe_w_up, moe_w_down, final_norm_w):
    bsz, seq, d = x.shape
    clen = ctx.shape[1]
    depth = ada_w.shape[0]
    m_lat, m_ctx = bsz * seq, bsz * clen

    lb_all = jnp.cumsum(jax.nn.softmax(hg_lb_logits.astype(F32), axis=0), axis=0)
    lb_all = lb_all - lb_all[0]
    cond = jnp.zeros((SUBLANES, d), F32).at[:bsz].set(c).at[bsz].set(c_ctx)
    cos, sin = _rope_tables(seq)

    e_hg = 5 * HG_WIDTH
    e_ssd = e_hg + SSM_WIDTH + SSM_CONV_CH
    e_dt = e_ssd + 2 * SSM_HEADS
    e_att = e_dt + ATT_WIDTH + 2 * ATT_KV_WIDTH

    zeros_hg = jnp.zeros((bsz, HG_HEADS, HG_DK, HG_DK), F32)
    zeros_ssd = jnp.zeros((bsz, SSM_HEADS // 2, 2 * SSM_HEAD_DIM, SSM_STATE), F32)

    x2 = x.reshape(m_lat, d)
    xc2 = ctx.reshape(m_ctx, d)
    for l in range(depth):
        need_ctx = l < depth - 1
        mod = ada_modulation(cond, ada_w[l], ada_b[l])
        mod_lat = mod[:bsz].reshape(bsz, 6, 1, d)
        mod_ctx = mod[bsz:bsz + 1].reshape(1, 6, 1, d)
        ml = [mod_lat[:, n] for n in range(6)]
        mc = [mod_ctx[:, n] for n in range(6)]

        wl = w_in[l]
        w_hg = wl[:, :e_hg].astype(BF16)
        w_ssd = wl[:, e_hg:e_ssd].astype(BF16)
        w_dt = jnp.zeros((d, LANES), BF16).at[:, :2 * SSM_HEADS].set(wl[:, e_ssd:e_dt].astype(BF16))
        w_att = wl[:, e_dt:e_att].astype(BF16)
        w_gate = wl[:, e_att:].astype(BF16)
        w_br = w_branch[l].astype(BF16)
        w_o = w_out[l].astype(BF16)

        prm = jnp.zeros((SUBLANES, LANES), F32)
        prm = prm.at[0, :2 * SSM_HEADS].set(ssm_dt_bias[l].reshape(-1).astype(F32))
        prm = prm.at[1, :2 * SSM_HEADS].set(-jnp.exp(ssm_a_log[l].reshape(-1).astype(F32)))
        prm = prm.at[2, :2 * SSM_HEADS].set(ssm_d[l].reshape(-1).astype(F32))

        def mixer_inputs(xs, nb, t, shift, scale):
            h = norm_modulate(xs.reshape(nb, t, d), norm_mix_w[l], shift, scale).reshape(nb * t, d)
            p_hg = matmul(h, w_hg, F32).reshape(nb, t, e_hg)
            p_ssd = matmul(h, w_ssd, F32).reshape(nb, t, e_ssd - e_hg)
            p_dt = matmul(h, w_dt, F32).reshape(nb, t, LANES)
            p_att = matmul(h, w_att, F32)
            xbc = ssd_conv_silu(p_ssd, ssm_conv_w[l], ssm_conv_b[l])
            return h, p_hg, p_ssd, p_dt, p_att, xbc

        h_c, phg_c, pssd_c, pdt_c, patt_c, xbc_c = mixer_inputs(xc2, bsz, clen, mc[0], mc[1])
        h_l, phg_l, pssd_l, pdt_l, patt_l, xbc_l = mixer_inputs(x2, bsz, seq, ml[0], ml[1])

        ofc, s_f = hgrn2_scan(phg_c, lb_all[l, 0], zeros_hg, False)
        obc, s_b = hgrn2_scan(phg_c, lb_all[l, 1], zeros_hg, True)
        ofl, _ = hgrn2_scan(phg_l, lb_all[l, 0], s_f, False)
        obl, _ = hgrn2_scan(phg_l, lb_all[l, 1], s_b, True)
        ya_l = hgrn2_readout(ofl.reshape(m_lat, -1), obl.reshape(m_lat, -1), phg_l.reshape(m_lat, -1), hg_norm_w[l])

        yfc, h_f = ssd_scan(xbc_c, pdt_c, prm, zeros_ssd, False)
        ybc, h_b = ssd_scan(xbc_c, pdt_c, prm, zeros_ssd, True)
        yfl, _ = ssd_scan(xbc_l, pdt_l, prm, h_f, False)
        ybl, _ = ssd_scan(xbc_l, pdt_l, prm, h_b, True)
        yb_l = ssd_readout(yfl.reshape(m_lat, -1), ybl.reshape(m_lat, -1), pssd_l.reshape(m_lat, -1), ssm_norm_w[l])

        q_c, k_c, v_c = attention_prep(patt_c, attn_q_norm_w[l], attn_k_norm_w[l], cos, sin, clen, False)
        q_l, k_l, v_l = attention_prep(patt_l, attn_q_norm_w[l], attn_k_norm_w[l], cos, sin, seq, True)
        k_c3, v_c3 = k_c.reshape(bsz, clen, -1), v_c.reshape(bsz, clen, -1)
        k_all = jnp.concatenate([k_c3, k_l.reshape(bsz, seq, -1)], axis=1)
        v_all = jnp.concatenate([v_c3, v_l.reshape(bsz, seq, -1)], axis=1)
        yc_l = gqa_attention(q_l.reshape(bsz, seq, -1), k_all, v_all).reshape(m_lat, -1)

        gates_l = matmul(h_l, w_gate, BF16, act="sigmoid")
        merged_l = merge_branches(ya_l, yb_l, yc_l, gates_l, w_br)
        x2 = matmul_residual(merged_l, w_o, x2, ml[2])

        if need_ctx:
            ya_c = hgrn2_readout(ofc.reshape(m_ctx, -1), obc.reshape(m_ctx, -1), phg_c.reshape(m_ctx, -1), hg_norm_w[l])
            yb_c = ssd_readout(yfc.reshape(m_ctx, -1), ybc.reshape(m_ctx, -1), pssd_c.reshape(m_ctx, -1), ssm_norm_w[l])
            yc_c = gqa_attention(q_c.reshape(bsz, clen, -1), k_c3, v_c3).reshape(m_ctx, -1)
            gates_c = matmul(h_c, w_gate, BF16, act="sigmoid")
            merged_c = merge_branches(ya_c, yb_c, yc_c, gates_c, w_br)
            xc2 = matmul_residual(merged_c, w_o, xc2, mc[2])

        if l % 2 == 0:
            wg = ffn_w_gate[l // 2].astype(BF16)
            wu = ffn_w_up[l // 2].astype(BF16)
            wd = ffn_w_down[l // 2].astype(BF16)
            hf = norm_modulate(x2.reshape(bsz, seq, d), norm_ffn_w[l], ml[3], ml[4]).reshape(m_lat, d)
            x2 = swiglu_residual(hf, wg, wu, wd, x2, ml[5])
            if need_ctx:
                hfc = norm_modulate(xc2.reshape(bsz, clen, d), norm_ffn_w[l], mc[3], mc[4]).reshape(m_ctx, d)
                xc2 = swiglu_residual(hfc, wg, wu, wd, xc2, mc[5])
        else:
            wg = moe_w_gate[l // 2].astype(BF16)
            wu = moe_w_up[l // 2].astype(BF16)
            wd = moe_w_down[l // 2].astype(BF16)
            x2 = moe_swiglu_residual(x2.reshape(bsz, seq, d), norm_ffn_w[l], ml[3], ml[4], moe_router[l // 2],
                                     wg, wu, wd, ml[5])
            if need_ctx:
                xc2 = moe_swiglu_residual(xc2.reshape(bsz, clen, d), norm_ffn_w[l],
                                          jnp.broadcast_to(mc[3], (bsz, 1, d)), jnp.broadcast_to(mc[4], (bsz, 1, d)),
                                          moe_router[l // 2], wg, wu, wd, mc[5])

    return rms_norm_rows(x2, final_norm_w).reshape(bsz, seq, d)
```

```python
import functools
import math

import jax
import jax.numpy as jnp
from jax import lax
from jax.experimental import pallas as pl
from jax.experimental.pallas import tpu as pltpu

F32 = jnp.float32
BF16 = jnp.bfloat16

EPS = 1e-6
GRID_W = 64
HG_HEADS = 8
HG_DK = 128
HG_WIDTH = HG_HEADS * HG_DK
HG_BLOCK = 16
SSM_HEADS = 16
SSM_HEAD_DIM = 64
SSM_WIDTH = SSM_HEADS * SSM_HEAD_DIM
SSM_GROUPS = 4
SSM_STATE = 128
SSM_CONV = 5
SSM_CONV_CH = SSM_WIDTH + 2 * SSM_GROUPS * SSM_STATE
SSM_CHUNK = 128
ATT_HEADS = 8
ATT_KV_HEADS = 2
ATT_HEAD_DIM = 128
ATT_GROUP = ATT_HEADS // ATT_KV_HEADS
ATT_WIDTH = ATT_HEADS * ATT_HEAD_DIM
ATT_KV_WIDTH = ATT_KV_HEADS * ATT_HEAD_DIM
LOG2E = math.log2(math.e)
ATT_Q_SCALE = ATT_HEAD_DIM ** -0.5 * LOG2E
ROPE_THETA = 10000.0
N_BRANCHES = 3
N_EXPERTS = 8
LANES = 128
SUBLANES = 8
VMEM_LIMIT = 56 * 1024 * 1024


def _params(*sem):
    return pltpu.CompilerParams(dimension_semantics=sem, vmem_limit_bytes=VMEM_LIMIT)


def _tile(n, prefs):
    for t in prefs:
        if n % t == 0:
            return t
    return n


def _sigmoid(x):
    return 1.0 / (1.0 + jnp.exp(-x))


def _silu(x):
    return x / (1.0 + jnp.exp(-x))


def _dot(a, b):
    return jnp.dot(a, b, preferred_element_type=F32)


def _dot_nt(a, b):
    return lax.dot_general(a, b, (((1,), (1,)), ((), ())), preferred_element_type=F32)


def _dot_tn(a, b):
    return lax.dot_general(a, b, (((0,), (0,)), ((), ())), preferred_element_type=F32)


def _ada_kernel(c_ref, w_ref, b_ref, o_ref):
    s = _silu(c_ref[...])
    o_ref[...] = jnp.dot(s, w_ref[...], preferred_element_type=F32,
                         precision=lax.Precision.HIGHEST) + b_ref[...]


def ada_modulation(cc, w, b):
    rows, d = cc.shape
    n = w.shape[1]
    tn = _tile(n, (1024, 512, 256, 128))
    return pl.pallas_call(
        _ada_kernel,
        out_shape=jax.ShapeDtypeStruct((rows, n), F32),
        grid=(n // tn,),
        in_specs=[pl.BlockSpec((rows, d), lambda j: (0, 0)),
                  pl.BlockSpec((d, tn), lambda j: (0, j)),
                  pl.BlockSpec((1, tn), lambda j: (0, j))],
        out_specs=pl.BlockSpec((rows, tn), lambda j: (0, j)),
        compiler_params=_params("arbitrary"),
        name="ada_modulation",
    )(cc, w, b.reshape(1, n))


def _normmod_kernel(x_ref, w_ref, sh_ref, sc_ref, o_ref):
    x = x_ref[0]
    ms = jnp.mean(x * x, axis=-1, keepdims=True)
    y = x * lax.rsqrt(ms + EPS) * w_ref[...]
    o_ref[0] = (y * (1.0 + sc_ref[0]) + sh_ref[0]).astype(o_ref.dtype)


def norm_modulate(x, w, shift, scale, out_dtype=BF16):
    b, t, d = x.shape
    tm = _tile(t, (512, 256, 128))
    bm = shift.shape[0]
    mod_map = (lambda i, j: (i, 0, 0)) if bm == b else (lambda i, j: (0, 0, 0))
    return pl.pallas_call(
        _normmod_kernel,
        out_shape=jax.ShapeDtypeStruct((b, t, d), out_dtype),
        grid=(b, t // tm),
        in_specs=[pl.BlockSpec((1, tm, d), lambda i, j: (i, j, 0)),
                  pl.BlockSpec((1, d), lambda i, j: (0, 0)),
                  pl.BlockSpec((1, 1, d), mod_map),
                  pl.BlockSpec((1, 1, d), mod_map)],
        out_specs=pl.BlockSpec((1, tm, d), lambda i, j: (i, j, 0)),
        compiler_params=_params("arbitrary", "arbitrary"),
        name="norm_modulate",
    )(x, w.reshape(1, d), shift, scale)


def _rmsnorm_kernel(x_ref, w_ref, o_ref):
    x = x_ref[...]
    ms = jnp.mean(x * x, axis=-1, keepdims=True)
    o_ref[...] = x * lax.rsqrt(ms + EPS) * w_ref[...]


def rms_norm_rows(x, w):
    m, d = x.shape
    tm = _tile(m, (512, 256, 128))
    return pl.pallas_call(
        _rmsnorm_kernel,
        out_shape=jax.ShapeDtypeStruct((m, d), F32),
        grid=(m // tm,),
        in_specs=[pl.BlockSpec((tm, d), lambda i: (i, 0)),
                  pl.BlockSpec((1, d), lambda i: (0, 0))],
        out_specs=pl.BlockSpec((tm, d), lambda i: (i, 0)),
        compiler_params=_params("arbitrary"),
        name="final_rms_norm",
    )(x, w.reshape(1, d))


def _mm_kernel(a_ref, w_ref, o_ref, *, act):
    acc = _dot(a_ref[...], w_ref[...])
    if act == "sigmoid":
        acc = _sigmoid(acc)
    o_ref[...] = acc.astype(o_ref.dtype)


def matmul(a, w, out_dtype, act=None):
    m, k = a.shape
    n = w.shape[1]
    tm = _tile(m, (1024, 512, 256, 128))
    tn = _tile(n, (1024, 512, 256, 128))
    return pl.pallas_call(
        functools.partial(_mm_kernel, act=act),
        out_shape=jax.ShapeDtypeStruct((m, n), out_dtype),
        grid=(m // tm, n // tn),
        in_specs=[pl.BlockSpec((tm, k), lambda i, j: (i, 0)),
                  pl.BlockSpec((k, tn), lambda i, j: (0, j))],
        out_specs=pl.BlockSpec((tm, tn), lambda i, j: (i, j)),
        compiler_params=_params("arbitrary", "arbitrary"),
        name="matmul_" + (act or "plain"),
    )(a, w)


def _mm_res_kernel(a_ref, w_ref, r_ref, m_ref, o_ref):
    o_ref[...] = r_ref[...] + m_ref[0] * _dot(a_ref[...], w_ref[...])


def matmul_residual(a, w, res, mod):
    m, k = a.shape
    n = w.shape[1]
    rows_per_mod = m // mod.shape[0]
    tm = _tile(rows_per_mod, (1024, 512, 256, 128))
    tn = _tile(n, (1024, 512, 256, 128))
    return pl.pallas_call(
        _mm_res_kernel,
        out_shape=jax.ShapeDtypeStruct((m, n), F32),
        grid=(m // tm, n // tn),
        in_specs=[pl.BlockSpec((tm, k), lambda i, j: (i, 0)),
                  pl.BlockSpec((k, tn), lambda i, j: (0, j)),
                  pl.BlockSpec((tm, tn), lambda i, j: (i, j)),
                  pl.BlockSpec((1, 1, tn), lambda i, j: (i * tm // rows_per_mod, 0, j))],
        out_specs=pl.BlockSpec((tm, tn), lambda i, j: (i, j)),
        compiler_params=_params("arbitrary", "arbitrary"),
        name="matmul_residual",
    )(a, w, res, mod)


def _merge_kernel(ya_ref, yb_ref, yc_ref, g0_ref, g1_ref, g2_ref, wb_ref, o_ref):
    acc = g0_ref[...].astype(F32) * _dot(ya_ref[...], wb_ref[0])
    acc = acc + g1_ref[...].astype(F32) * _dot(yb_ref[...], wb_ref[1])
    acc = acc + g2_ref[...].astype(F32) * _dot(yc_ref[...], wb_ref[2])
    o_ref[...] = acc.astype(o_ref.dtype)


def merge_branches(ya, yb, yc, gates, wb):
    m, kb = ya.shape
    d = wb.shape[2]
    tm = _tile(m, (1024, 512, 256, 128))
    tn = _tile(d, (512, 256, 128))
    nj = d // tn
    y_spec = pl.BlockSpec((tm, kb), lambda i, j: (i, 0))

    def g_spec(nb):
        return pl.BlockSpec((tm, tn), lambda i, j: (i, nb * nj + j))

    return pl.pallas_call(
        _merge_kernel,
        out_shape=jax.ShapeDtypeStruct((m, d), BF16),
        grid=(m // tm, nj),
        in_specs=[y_spec, y_spec, y_spec, g_spec(0), g_spec(1), g_spec(2),
                  pl.BlockSpec((N_BRANCHES, kb, tn), lambda i, j: (0, 0, j))],
        out_specs=pl.BlockSpec((tm, tn), lambda i, j: (i, j)),
        compiler_params=_params("arbitrary", "arbitrary"),
        name="merge_branches",
    )(ya, yb, yc, gates, gates, gates, wb)


def _ffn_kernel(h_ref, wg_ref, wu_ref, wd_ref, x_ref, m_ref, o_ref, acc_ref):
    f = pl.program_id(1)

    @pl.when(f == 0)
    def _():
        acc_ref[...] = jnp.zeros_like(acc_ref)

    h = h_ref[...]
    a = _silu(_dot(h, wg_ref[...])) * _dot(h, wu_ref[...])
    acc_ref[...] += _dot(a.astype(BF16), wd_ref[...])

    @pl.when(f == pl.num_programs(1) - 1)
    def _():
        o_ref[...] = x_ref[...] + m_ref[0] * acc_ref[...]


def swiglu_residual(h, wg, wu, wd, x, mod):
    m, d = h.shape
    ff = wg.shape[1]
    rows_per_mod = m // mod.shape[0]
    tm = _tile(rows_per_mod, (512, 256, 128))
    tf = _tile(ff, (512, 256, 128))
    return pl.pallas_call(
        _ffn_kernel,
        out_shape=jax.ShapeDtypeStruct((m, d), F32),
        grid=(m // tm, ff // tf),
        in_specs=[pl.BlockSpec((tm, d), lambda i, f: (i, 0)),
                  pl.BlockSpec((d, tf), lambda i, f: (0, f)),
                  pl.BlockSpec((d, tf), lambda i, f: (0, f)),
                  pl.BlockSpec((tf, d), lambda i, f: (f, 0)),
                  pl.BlockSpec((tm, d), lambda i, f: (i, 0)),
                  pl.BlockSpec((1, 1, d), lambda i, f: (i * tm // rows_per_mod, 0, 0))],
        out_specs=pl.BlockSpec((tm, d), lambda i, f: (i, 0)),
        scratch_shapes=[pltpu.VMEM((tm, d), F32)],
        compiler_params=_params("arbitrary", "arbitrary"),
        name="swiglu_residual",
    )(h, wg, wu, wd, x, mod)


MOE_TILE = 512
ROUTE_E0, ROUTE_E1, ROUTE_W0, ROUTE_W1, ROUTE_P0, ROUTE_P1 = range(6)


def _router_kernel(x_ref, w_ref, sh_ref, sc_ref, rw_ref, h_ref, r_ref, cnt_ref, run_ref):
    @pl.when((pl.program_id(0) == 0) & (pl.program_id(1) == 0))
    def _():
        run_ref[...] = jnp.zeros_like(run_ref)

    x = x_ref[0]
    ms = jnp.mean(x * x, axis=-1, keepdims=True)
    h = x * lax.rsqrt(ms + EPS) * w_ref[...] * (1.0 + sc_ref[0]) + sh_ref[0]
    h_ref[0] = h
    logits = jnp.dot(h, rw_ref[...], preferred_element_type=F32, precision=lax.Precision.HIGHEST)
    lane = lax.broadcasted_iota(jnp.int32, logits.shape, 1)
    logits = jnp.where(lane < N_EXPERTS, logits, -jnp.inf)
    m1 = jnp.max(logits, axis=-1, keepdims=True)
    i1 = jnp.min(jnp.where(logits == m1, lane, LANES), axis=-1, keepdims=True)
    rest = jnp.where(lane == i1, -jnp.inf, logits)
    m2 = jnp.max(rest, axis=-1, keepdims=True)
    i2 = jnp.min(jnp.where(rest == m2, lane, LANES), axis=-1, keepdims=True)
    e2 = jnp.exp(m2 - m1)
    den = 1.0 + e2
    sel1 = lane == i1
    sel2 = lane == i2
    onehot = jnp.where(sel1 | sel2, 1.0, 0.0)
    tm = onehot.shape[0]
    tri = jnp.where(lax.broadcasted_iota(jnp.int32, (tm, tm), 0) > lax.broadcasted_iota(jnp.int32, (tm, tm), 1),
                    1.0, 0.0).astype(BF16)
    ranks = _dot(tri, onehot.astype(BF16)) + run_ref[...]
    p1 = jnp.sum(jnp.where(sel1, ranks, 0.0), axis=-1, keepdims=True)
    p2 = jnp.sum(jnp.where(sel2, ranks, 0.0), axis=-1, keepdims=True)
    run_ref[...] += jnp.sum(onehot, axis=0, keepdims=True)
    cnt_ref[...] = jnp.broadcast_to(run_ref[...], cnt_ref.shape)
    rec = jnp.where(lane == ROUTE_E0, i1.astype(F32), 0.0)
    rec = jnp.where(lane == ROUTE_E1, i2.astype(F32), rec)
    rec = jnp.where(lane == ROUTE_W0, 1.0 / den, rec)
    rec = jnp.where(lane == ROUTE_W1, e2 / den, rec)
    rec = jnp.where(lane == ROUTE_P0, p1, rec)
    rec = jnp.where(lane == ROUTE_P1, p2, rec)
    r_ref[0] = rec


def route_top2(x, w, shift, scale, router_w):
    b, t, d = x.shape
    tm = _tile(t, (512, 256, 128))
    rw = jnp.zeros((d, LANES), F32).at[:, :N_EXPERTS].set(router_w)
    return pl.pallas_call(
        _router_kernel,
        out_shape=(jax.ShapeDtypeStruct((b, t, d), F32), jax.ShapeDtypeStruct((b, t, LANES), F32),
                   jax.ShapeDtypeStruct((SUBLANES, LANES), F32)),
        grid=(b, t // tm),
        in_specs=[pl.BlockSpec((1, tm, d), lambda i, j: (i, j, 0)),
                  pl.BlockSpec((1, d), lambda i, j: (0, 0)),
                  pl.BlockSpec((1, 1, d), lambda i, j: (i, 0, 0)),
                  pl.BlockSpec((1, 1, d), lambda i, j: (i, 0, 0)),
                  pl.BlockSpec((d, LANES), lambda i, j: (0, 0))],
        out_specs=(pl.BlockSpec((1, tm, d), lambda i, j: (i, j, 0)),
                   pl.BlockSpec((1, tm, LANES), lambda i, j: (i, j, 0)),
                   pl.BlockSpec((SUBLANES, LANES), lambda i, j: (0, 0))),
        scratch_shapes=[pltpu.VMEM((1, LANES), F32)],
        compiler_params=_params("arbitrary", "arbitrary"),
        name="moe_router",
    )(x, w.reshape(1, d), shift, scale, rw)


def _row_copy(src, src_row, dst, dst_row, sem):
    return pltpu.make_async_copy(src.at[pl.ds(src_row, 1)], dst.at[pl.ds(dst_row, 1)], sem)


def _dispatch_kernel(d0_ref, d1_ref, h_ref, init_hbm, hs_hbm, sem, *, tm):
    del init_hbm
    base = pl.program_id(0) * tm

    def issue(r, carry):
        t = base + r
        _row_copy(h_ref, r, hs_hbm, d0_ref[t], sem).start()
        _row_copy(h_ref, r, hs_hbm, d1_ref[t], sem).start()
        return carry

    def drain(r, carry):
        _row_copy(h_ref, 0, hs_hbm, 0, sem).wait()
        _row_copy(h_ref, 0, hs_hbm, 0, sem).wait()
        return carry

    lax.fori_loop(0, tm, issue, 0)
    lax.fori_loop(0, tm, drain, 0)


def moe_dispatch(h, dest0, dest1, n_rows):
    m, d = h.shape
    tm = _tile(m, (256, 128))
    any_spec = pl.BlockSpec(memory_space=pl.ANY)
    return pl.pallas_call(
        functools.partial(_dispatch_kernel, tm=tm),
        out_shape=jax.ShapeDtypeStruct((n_rows, d), F32),
        grid_spec=pltpu.PrefetchScalarGridSpec(
            num_scalar_prefetch=2, grid=(m // tm,),
            in_specs=[pl.BlockSpec((tm, d), lambda i, d0, d1: (i, 0)), any_spec], out_specs=any_spec,
            scratch_shapes=[pltpu.SemaphoreType.DMA]),
        input_output_aliases={3: 0},
        compiler_params=_params("arbitrary"),
        name="moe_dispatch",
    )(dest0, dest1, h, jnp.zeros((n_rows, d), F32))


def _grouped_ffn_kernel(te_ref, nu_ref, hs_ref, wg_ref, wu_ref, wd_ref, o_ref, acc_ref, hb_ref):
    del te_ref
    i = pl.program_id(0)
    f = pl.program_id(1)

    @pl.when(f == 0)
    def _():
        acc_ref[...] = jnp.zeros_like(acc_ref)
        hb_ref[...] = hs_ref[...].astype(BF16)

    @pl.when(i < nu_ref[0])
    def _():
        h = hb_ref[...]
        a = _silu(_dot(h, wg_ref[0])) * _dot(h, wu_ref[0])
        acc_ref[...] += _dot(a.astype(BF16), wd_ref[0])

    @pl.when(f == pl.num_programs(1) - 1)
    def _():
        o_ref[...] = acc_ref[...]


def moe_grouped_swiglu(hs, tile_expert, n_used, wg, wu, wd):
    p, d = hs.shape
    ff = wg.shape[2]
    tm = MOE_TILE
    tf = _tile(ff, (512, 256, 128))

    def fblock(i, f, nu):
        return jnp.where(i < nu[0], f, 0)

    return pl.pallas_call(
        _grouped_ffn_kernel,
        out_shape=jax.ShapeDtypeStruct((p, d), F32),
        grid_spec=pltpu.PrefetchScalarGridSpec(
            num_scalar_prefetch=2, grid=(p // tm, ff // tf),
            in_specs=[pl.BlockSpec((tm, d), lambda i, f, te, nu: (i, 0)),
                      pl.BlockSpec((1, d, tf), lambda i, f, te, nu: (te[i], 0, fblock(i, f, nu))),
                      pl.BlockSpec((1, d, tf), lambda i, f, te, nu: (te[i], 0, fblock(i, f, nu))),
                      pl.BlockSpec((1, tf, d), lambda i, f, te, nu: (te[i], fblock(i, f, nu), 0))],
            out_specs=pl.BlockSpec((tm, d), lambda i, f, te, nu: (i, 0)),
            scratch_shapes=[pltpu.VMEM((tm, d), F32), pltpu.VMEM((tm, d), BF16)]),
        compiler_params=_params("arbitrary", "arbitrary"),
        name="moe_grouped_swiglu",
    )(tile_expert, n_used, hs, wg, wu, wd)


def _combine_kernel(d0_ref, d1_ref, ys_hbm, r_ref, x_ref, m_ref, o_ref, buf_ref, sem, *, tm):
    base = pl.program_id(0) * tm

    def issue(r, carry):
        t = base + r
        _row_copy(ys_hbm, d0_ref[t], buf_ref.at[0], r, sem).start()
        _row_copy(ys_hbm, d1_ref[t], buf_ref.at[1], r, sem).start()
        return carry

    def drain(r, carry):
        _row_copy(ys_hbm, 0, buf_ref.at[0], 0, sem).wait()
        _row_copy(ys_hbm, 0, buf_ref.at[1], 0, sem).wait()
        return carry

    lax.fori_loop(0, tm, issue, 0)
    lax.fori_loop(0, tm, drain, 0)
    rec = r_ref[...]
    lane = lax.broadcasted_iota(jnp.int32, rec.shape, 1)
    w0 = jnp.sum(jnp.where(lane == ROUTE_W0, rec, 0.0), axis=-1, keepdims=True)
    w1 = jnp.sum(jnp.where(lane == ROUTE_W1, rec, 0.0), axis=-1, keepdims=True)
    o_ref[...] = x_ref[...] + m_ref[0] * (w0 * buf_ref[0] + w1 * buf_ref[1])


def moe_combine(ys, dest0, dest1, route, x, mod):
    m, d = x.shape
    rows_per_mod = m // mod.shape[0]
    tm = _tile(rows_per_mod, (256, 128))
    return pl.pallas_call(
        functools.partial(_combine_kernel, tm=tm),
        out_shape=jax.ShapeDtypeStruct((m, d), F32),
        grid_spec=pltpu.PrefetchScalarGridSpec(
            num_scalar_prefetch=2, grid=(m // tm,),
            in_specs=[pl.BlockSpec(memory_space=pl.ANY),
                      pl.BlockSpec((tm, LANES), lambda i, d0, d1: (i, 0)),
                      pl.BlockSpec((tm, d), lambda i, d0, d1: (i, 0)),
                      pl.BlockSpec((1, 1, d), lambda i, d0, d1: (i * tm // rows_per_mod, 0, 0))],
            out_specs=pl.BlockSpec((tm, d), lambda i, d0, d1: (i, 0)),
            scratch_shapes=[pltpu.VMEM((2, tm, d), F32), pltpu.SemaphoreType.DMA]),
        compiler_params=_params("arbitrary"),
        name="moe_combine",
    )(dest0, dest1, ys, route, x, mod)


def moe_swiglu_residual(x3, norm_w, shift, scale, router_w, wg, wu, wd, mod):
    b, t, d = x3.shape
    m = b * t
    h, route, counts = route_top2(x3, norm_w, shift, scale, router_w)
    route = route.reshape(m, LANES)
    cnt = counts[0, :N_EXPERTS].astype(jnp.int32)
    padded = (cnt + MOE_TILE - 1) // MOE_TILE * MOE_TILE
    ends = jnp.cumsum(padded)
    starts = ends - padded
    e0 = route[:, ROUTE_E0].astype(jnp.int32)
    e1 = route[:, ROUTE_E1].astype(jnp.int32)
    dest0 = starts[e0] + route[:, ROUTE_P0].astype(jnp.int32)
    dest1 = starts[e1] + route[:, ROUTE_P1].astype(jnp.int32)
    n_rows = 2 * m + N_EXPERTS * MOE_TILE
    n_tiles = n_rows // MOE_TILE
    n_used = ends[-1] // MOE_TILE
    tile_ids = jnp.minimum(jnp.arange(n_tiles, dtype=jnp.int32), n_used - 1)
    tile_expert = jnp.sum((tile_ids[:, None] >= (ends // MOE_TILE)[None, :]).astype(jnp.int32), axis=1)
    tile_expert = jnp.minimum(tile_expert, N_EXPERTS - 1).astype(jnp.int32)
    hs = moe_dispatch(h.reshape(m, d), dest0, dest1, n_rows)
    ys = moe_grouped_swiglu(hs, tile_expert, n_used.reshape(1).astype(jnp.int32), wg, wu, wd)
    return moe_combine(ys, dest0, dest1, route, x3.reshape(m, d), mod)


def _hgrn2_kernel(q_ref, f_ref, v_ref, lb_ref, s0_ref, o_ref, sout_ref, st_ref, c_ref, k_ref,
                  *, reverse, rows):
    j = pl.program_id(1)

    @pl.when(j == 0)
    def _():
        st_ref[...] = s0_ref[0]

    lb = lb_ref[...]
    fr = f_ref[0]
    softplus_neg = jnp.log(1.0 + jnp.exp(-jnp.abs(fr)))
    log_sig = jnp.minimum(fr, 0.0) - softplus_neg
    a = jnp.log(lb)
    b = jnp.log(1.0 - lb) + log_sig
    log_f = jnp.maximum(a, b) + jnp.log(1.0 + jnp.exp(-jnp.abs(a - b)))
    log_k = jnp.log(1.0 - lb) + jnp.minimum(-fr, 0.0) - softplus_neg
    rib = lax.broadcasted_iota(jnp.int32, fr.shape, 0) & (HG_BLOCK - 1)
    c = log_f
    sh = 1
    while sh < HG_BLOCK:
        if reverse:
            c = c + jnp.where(rib < HG_BLOCK - sh, pltpu.roll(c, rows - sh, axis=0), 0.0)
        else:
            c = c + jnp.where(rib >= sh, pltpu.roll(c, sh, axis=0), 0.0)
        sh *= 2
    c = c * LOG2E
    c_ref[...] = c
    k_ref[...] = c - log_k * LOG2E

    trow = lax.broadcasted_iota(jnp.int32, (HG_BLOCK, HG_DK), 0)
    blocks = list(range(rows // HG_BLOCK))
    if reverse:
        blocks = blocks[::-1]
    pw = 2 * HG_DK

    zblk = jnp.zeros((HG_DK, HG_DK), BF16)

    def head_pair(p):
        cols = slice(p * pw, (p + 1) * pw)
        s_a = st_ref[2 * p]
        s_b = st_ref[2 * p + 1]
        for blk in blocks:
            r0 = blk * HG_BLOCK
            qb = q_ref[0, r0:r0 + HG_BLOCK, cols] * (HG_DK ** -0.5)
            vb = v_ref[0, r0:r0 + HG_BLOCK, cols]
            db = k_ref[r0:r0 + HG_BLOCK, cols]
            cb = c_ref[r0:r0 + HG_BLOCK, cols]
            wblk = jnp.concatenate([jnp.concatenate([s_a.astype(BF16), zblk], axis=1),
                                    jnp.concatenate([zblk, s_b.astype(BF16)], axis=1)], axis=0)
            o = _dot_nt((qb * jnp.exp2(cb)).astype(BF16), wblk)
            diag = []
            for hk in range(2):
                sl = slice(hk * HG_DK, (hk + 1) * HG_DK)
                qh, dh, vh, ch = qb[:, sl], db[:, sl], vb[:, sl], cb[:, sl]
                oh = jnp.zeros((HG_BLOCK, HG_DK), F32)
                for jj in range(HG_BLOCK):
                    valid = (trow <= jj) if reverse else (trow >= jj)
                    e = jnp.exp2(jnp.where(valid, ch - dh[jj:jj + 1], -jnp.inf)) * qh
                    oh = oh + jnp.sum(e, axis=-1, keepdims=True) * vh[jj:jj + 1]
                diag.append(oh)
            o_ref[0, r0:r0 + HG_BLOCK, cols] = o + jnp.concatenate(diag, axis=1)
            c_tot = cb[0:1] if reverse else cb[HG_BLOCK - 1:HG_BLOCK]
            kt = jnp.exp2(c_tot - db).astype(BF16)
            upd = _dot_tn(vb.astype(BF16), kt)
            grow = jnp.exp2(c_tot)
            s_a = s_a * grow[:, :HG_DK] + upd[:HG_DK, :HG_DK]
            s_b = s_b * grow[:, HG_DK:] + upd[HG_DK:, HG_DK:]
        st_ref[2 * p] = s_a
        st_ref[2 * p + 1] = s_b

    for p in range(HG_HEADS // 2):
        head_pair(p)

    @pl.when(j == pl.num_programs(1) - 1)
    def _():
        sout_ref[0] = st_ref[...]


def hgrn2_scan(p_hg, lb, s0, reverse):
    b, t, _ = p_hg.shape
    w = HG_WIDTH
    rows = _tile(t, (128, 64, 32, 16))
    nt = t // rows
    tmap = (lambda j: nt - 1 - j) if reverse else (lambda j: j)
    fcol = 2 if reverse else 1
    return pl.pallas_call(
        functools.partial(_hgrn2_kernel, reverse=reverse, rows=rows),
        out_shape=(jax.ShapeDtypeStruct((b, t, w), F32),
                   jax.ShapeDtypeStruct(s0.shape, F32)),
        grid=(b, nt),
        in_specs=[pl.BlockSpec((1, rows, w), lambda i, j: (i, tmap(j), 0)),
                  pl.BlockSpec((1, rows, w), lambda i, j: (i, tmap(j), fcol)),
                  pl.BlockSpec((1, rows, w), lambda i, j: (i, tmap(j), 3)),
                  pl.BlockSpec((1, w), lambda i, j: (0, 0)),
                  pl.BlockSpec((1, HG_HEADS, HG_DK, HG_DK), lambda i, j: (i, 0, 0, 0))],
        out_specs=(pl.BlockSpec((1, rows, w), lambda i, j: (i, tmap(j), 0)),
                   pl.BlockSpec((1, HG_HEADS, HG_DK, HG_DK), lambda i, j: (i, 0, 0, 0))),
        scratch_shapes=[pltpu.VMEM((HG_HEADS, HG_DK, HG_DK), F32),
                        pltpu.VMEM((rows, w), F32),
                        pltpu.VMEM((rows, w), F32)],
        compiler_params=_params("arbitrary", "arbitrary"),
        name="hgrn2_scan_bwd" if reverse else "hgrn2_scan_fwd",
    )(p_hg, p_hg, p_hg, lb.reshape(1, w), s0)


def _hg_readout_kernel(of_ref, ob_ref, g_ref, w_ref, y_ref):
    for h in range(HG_HEADS):
        sl = slice(h * HG_DK, (h + 1) * HG_DK)
        o = of_ref[:, sl] + ob_ref[:, sl]
        ms = jnp.mean(o * o, axis=-1, keepdims=True)
        y = o * lax.rsqrt(ms + EPS) * w_ref[:, sl]
        y_ref[:, sl] = (y * _silu(g_ref[:, sl])).astype(y_ref.dtype)


def hgrn2_readout(o_f, o_b, p_hg, norm_w):
    m, w = o_f.shape
    tm = _tile(m, (512, 256, 128))
    spec = pl.BlockSpec((tm, w), lambda i: (i, 0))
    return pl.pallas_call(
        _hg_readout_kernel,
        out_shape=jax.ShapeDtypeStruct((m, w), BF16),
        grid=(m // tm,),
        in_specs=[spec, spec, pl.BlockSpec((tm, w), lambda i: (i, 4)),
                  pl.BlockSpec((1, w), lambda i: (0, 0))],
        out_specs=spec,
        compiler_params=_params("arbitrary"),
        name="hgrn2_readout",
    )(o_f, o_b, p_hg, norm_w.reshape(1, w))


def _conv_kernel(prev_ref, cur_ref, next_ref, w_ref, b_ref, o_ref, *, tm):
    i = pl.program_id(1)
    prev = jnp.where(i > 0, prev_ref[0], 0.0)
    nxt = jnp.where(i < pl.num_programs(1) - 1, next_ref[0], 0.0)
    xe = jnp.concatenate([prev, cur_ref[0], nxt], axis=0)
    ext = tm + 2 * SUBLANES
    acc = jnp.zeros((tm, xe.shape[1]), F32) + b_ref[...]
    for k in range(SSM_CONV):
        sh = (SSM_CONV // 2 - k) % ext
        xs = xe if sh == 0 else pltpu.roll(xe, sh, axis=0)
        acc = acc + w_ref[k:k + 1, :] * xs[SUBLANES:SUBLANES + tm]
    o_ref[0] = _silu(acc)


def ssd_conv_silu(p_ssd, conv_w, conv_b):
    b, t, _ = p_ssd.shape
    ch = SSM_CONV_CH
    tm = _tile(t, (512, 256, 128))
    tc = 1024
    c0 = SSM_WIDTH // tc
    hb = tm // SUBLANES
    nh = t // SUBLANES
    return pl.pallas_call(
        functools.partial(_conv_kernel, tm=tm),
        out_shape=jax.ShapeDtypeStruct((b, t, ch), F32),
        grid=(b, t // tm, ch // tc),
        in_specs=[pl.BlockSpec((1, SUBLANES, tc), lambda i, j, cj: (i, jnp.maximum(j * hb - 1, 0), c0 + cj)),
                  pl.BlockSpec((1, tm, tc), lambda i, j, cj: (i, j, c0 + cj)),
                  pl.BlockSpec((1, SUBLANES, tc), lambda i, j, cj: (i, jnp.minimum((j + 1) * hb, nh - 1), c0 + cj)),
                  pl.BlockSpec((SUBLANES, tc), lambda i, j, cj: (0, cj)),
                  pl.BlockSpec((1, tc), lambda i, j, cj: (0, cj))],
        out_specs=pl.BlockSpec((1, tm, tc), lambda i, j, cj: (i, j, cj)),
        compiler_params=_params("arbitrary", "arbitrary", "arbitrary"),
        name="ssd_conv_silu",
    )(p_ssd, p_ssd, p_ssd,
      jnp.zeros((SUBLANES, ch), F32).at[:SSM_CONV].set(conv_w), conv_b.reshape(1, ch))


def _split3(x):
    hi = x.astype(BF16)
    r1 = x - hi.astype(F32)
    mid = r1.astype(BF16)
    lo = (r1 - mid.astype(F32)).astype(BF16)
    return hi, mid, lo


def _pick_lanes(parts, sel):
    return _dot(parts[0], sel) + _dot(parts[1], sel) + _dot(parts[2], sel)


def _ssd_kernel(x_ref, b_ref, c_ref, dt_ref, prm_ref, selh_ref, selp_ref, h0_ref, y_ref, hout_ref, hst_ref,
                *, reverse):
    j = pl.program_id(1)
    cl = SSM_CHUNK
    hd = SSM_HEAD_DIM

    @pl.when(j == 0)
    def _():
        hst_ref[...] = h0_ref[0]

    prm = prm_ref[...]
    dtr = dt_ref[0] + prm[0:1]
    dt = jnp.maximum(dtr, 0.0) + jnp.log(1.0 + jnp.exp(-jnp.abs(dtr)))
    a = dt * prm[1:2]
    row = lax.broadcasted_iota(jnp.int32, a.shape, 0)
    ac = a
    sh = 1
    while sh < cl:
        if reverse:
            ac = ac + jnp.where(row < cl - sh, pltpu.roll(ac, cl - sh, axis=0), 0.0)
        else:
            ac = ac + jnp.where(row >= sh, pltpu.roll(ac, sh, axis=0), 0.0)
        sh *= 2
    ac_t = ac.T
    ac_parts = _split3(ac)
    ac_head = _pick_lanes(ac_parts, selh_ref[...])
    ac_x = _pick_lanes(ac_parts, selp_ref[...])
    dt_x = _pick_lanes(_split3(dt), selp_ref[...])
    d_x = _pick_lanes(_split3(prm), selp_ref[...])[2:3]
    tot_x = ac_x[0:1] if reverse else ac_x[cl - 1:cl]
    xs = x_ref[0]
    xdt = xs * dt_x
    y_skip = xs * d_x
    exp_ac = jnp.exp(ac_x)
    xdec = (xdt * jnp.exp(tot_x - ac_x)).astype(BF16)
    exp_tot = jnp.exp(tot_x)
    xdt = xdt.astype(BF16)

    ti = lax.broadcasted_iota(jnp.int32, (cl, cl), 0)
    si = lax.broadcasted_iota(jnp.int32, (cl, cl), 1)
    mask = (ti <= si) if reverse else (ti >= si)
    first_head = si < hd
    zero = jnp.zeros((), BF16)
    lane0 = SSM_HEADS if reverse else 0
    pairs_per_group = SSM_HEADS // SSM_GROUPS // 2
    for g in range(SSM_GROUPS):
        bg = b_ref[0, :, g * SSM_STATE:(g + 1) * SSM_STATE].astype(BF16)
        cg = c_ref[0, :, g * SSM_STATE:(g + 1) * SSM_STATE].astype(BF16)
        cb = _dot_nt(cg, bg)
        for r in range(pairs_per_group):
            p = g * pairs_per_group + r
            cols = slice(p * 2 * hd, (p + 1) * 2 * hd)
            xp = xdt[:, cols]
            y = None
            for k, keep in ((0, first_head), (1, ~first_head)):
                h = 2 * p + k
                arow = ac_t[lane0 + h:lane0 + h + 1, :]
                acol = ac_head[:, h * LANES:(h + 1) * LANES]
                w = cb * jnp.exp(jnp.where(mask, acol - arow, -jnp.inf))
                yk = _dot(w.astype(BF16), jnp.where(keep, xp, zero))
                y = yk if y is None else y + yk
            hp = hst_ref[p]
            y = y + exp_ac[:, cols] * _dot_nt(cg, hp.astype(BF16)) + y_skip[:, cols]
            grow = jnp.where(ti < hd, exp_tot[:, p * 2 * hd:p * 2 * hd + 1], exp_tot[:, (p + 1) * 2 * hd - 1:(p + 1) * 2 * hd])
            hst_ref[p] = hp * grow + _dot_tn(xdec[:, cols], bg)
            y_ref[0, :, cols] = y

    @pl.when(j == pl.num_programs(1) - 1)
    def _():
        hout_ref[0] = hst_ref[...]


def ssd_scan(xbc, dt_raw, prm, h0, reverse):
    b, t, _ = xbc.shape
    cl = SSM_CHUNK
    nt = t // cl
    gw = SSM_GROUPS * SSM_STATE
    tmap = (lambda j: nt - 1 - j) if reverse else (lambda j: j)
    lane0 = SSM_HEADS if reverse else 0
    src = jnp.arange(LANES, dtype=jnp.int32)[:, None]
    sel_head = (src == lane0 + jnp.arange(SSM_HEADS * LANES, dtype=jnp.int32)[None, :] // LANES).astype(BF16)
    sel_x = (src == lane0 + jnp.arange(SSM_WIDTH, dtype=jnp.int32)[None, :] // SSM_HEAD_DIM).astype(BF16)
    st_spec = pl.BlockSpec((1,) + h0.shape[1:], lambda i, j: (i, 0, 0, 0))
    return pl.pallas_call(
        functools.partial(_ssd_kernel, reverse=reverse),
        out_shape=(jax.ShapeDtypeStruct((b, t, SSM_WIDTH), F32),
                   jax.ShapeDtypeStruct(h0.shape, F32)),
        grid=(b, nt),
        in_specs=[pl.BlockSpec((1, cl, SSM_WIDTH), lambda i, j: (i, tmap(j), 0)),
                  pl.BlockSpec((1, cl, gw), lambda i, j: (i, tmap(j), SSM_WIDTH // gw)),
                  pl.BlockSpec((1, cl, gw), lambda i, j: (i, tmap(j), SSM_WIDTH // gw + 1)),
                  pl.BlockSpec((1, cl, LANES), lambda i, j: (i, tmap(j), 0)),
                  pl.BlockSpec((SUBLANES, LANES), lambda i, j: (0, 0)),
                  pl.BlockSpec(sel_head.shape, lambda i, j: (0, 0)),
                  pl.BlockSpec(sel_x.shape, lambda i, j: (0, 0)),
                  st_spec],
        out_specs=(pl.BlockSpec((1, cl, SSM_WIDTH), lambda i, j: (i, tmap(j), 0)), st_spec),
        scratch_shapes=[pltpu.VMEM(h0.shape[1:], F32)],
        compiler_params=_params("arbitrary", "arbitrary"),
        name="ssd_scan_bwd" if reverse else "ssd_scan_fwd",
    )(xbc, xbc, xbc, dt_raw, prm, sel_head, sel_x, h0)


def _ssd_readout_kernel(yf_ref, yb_ref, z_ref, w_ref, o_ref):
    gw = SSM_WIDTH // SSM_GROUPS
    for g in range(SSM_GROUPS):
        sl = slice(g * gw, (g + 1) * gw)
        y = (yf_ref[:, sl] + yb_ref[:, sl]) * _silu(z_ref[:, sl])
        ms = jnp.mean(y * y, axis=-1, keepdims=True)
        o_ref[:, sl] = (y * lax.rsqrt(ms + EPS) * w_ref[:, sl]).astype(o_ref.dtype)


def ssd_readout(y_f, y_b, p_ssd, norm_w):
    m, w = y_f.shape
    tm = _tile(m, (512, 256, 128))
    spec = pl.BlockSpec((tm, w), lambda i: (i, 0))
    return pl.pallas_call(
        _ssd_readout_kernel,
        out_shape=jax.ShapeDtypeStruct((m, w), BF16),
        grid=(m // tm,),
        in_specs=[spec, spec, spec, pl.BlockSpec((1, w), lambda i: (0, 0))],
        out_specs=spec,
        compiler_params=_params("arbitrary"),
        name="ssd_readout",
    )(y_f, y_b, p_ssd, norm_w.reshape(1, w))


def _attprep_kernel(p_ref, qw_ref, kw_ref, cos_ref, sin_ref, q_ref, k_ref, v_ref, *, rope):
    lane = lax.broadcasted_iota(jnp.int32, (p_ref.shape[0], ATT_HEAD_DIM), 1)
    first = (lane % (ATT_HEAD_DIM // 2)) < (ATT_HEAD_DIM // 4)
    for h in range(ATT_HEADS + ATT_KV_HEADS):
        xh = p_ref[:, h * ATT_HEAD_DIM:(h + 1) * ATT_HEAD_DIM]
        w = qw_ref[...] if h < ATT_HEADS else kw_ref[...]
        ms = jnp.mean(xh * xh, axis=-1, keepdims=True)
        y = xh * lax.rsqrt(ms + EPS) * w
        if rope:
            partner = jnp.where(first, pltpu.roll(y, 3 * ATT_HEAD_DIM // 4, axis=1),
                                pltpu.roll(y, ATT_HEAD_DIM // 4, axis=1))
            y = y * cos_ref[...] + partner * sin_ref[...]
        if h < ATT_HEADS:
            q_ref[:, h * ATT_HEAD_DIM:(h + 1) * ATT_HEAD_DIM] = (y * ATT_Q_SCALE).astype(q_ref.dtype)
        else:
            hk = h - ATT_HEADS
            k_ref[:, hk * ATT_HEAD_DIM:(hk + 1) * ATT_HEAD_DIM] = y.astype(k_ref.dtype)
            v0 = ATT_WIDTH + ATT_KV_WIDTH + hk * ATT_HEAD_DIM
            v_ref[:, 2 * hk * ATT_HEAD_DIM:(2 * hk + 1) * ATT_HEAD_DIM] = p_ref[:, v0:v0 + ATT_HEAD_DIM].astype(v_ref.dtype)
            v_ref[:, (2 * hk + 1) * ATT_HEAD_DIM:(2 * hk + 2) * ATT_HEAD_DIM] = jnp.ones(
                (p_ref.shape[0], ATT_HEAD_DIM), v_ref.dtype)


def attention_prep(p_att, q_norm_w, k_norm_w, cos, sin, seq, rope):
    m, wtot = p_att.shape
    tm = _tile(seq, (512, 256, 128))
    npos = seq // tm
    tab = pl.BlockSpec((tm, ATT_HEAD_DIM), (lambda i: (i % npos, 0)) if rope else (lambda i: (0, 0)))
    wspec = pl.BlockSpec((1, ATT_HEAD_DIM), lambda i: (0, 0))
    return pl.pallas_call(
        functools.partial(_attprep_kernel, rope=rope),
        out_shape=(jax.ShapeDtypeStruct((m, ATT_WIDTH), BF16),
                   jax.ShapeDtypeStruct((m, ATT_KV_WIDTH), BF16),
                   jax.ShapeDtypeStruct((m, 2 * ATT_KV_WIDTH), BF16)),
        grid=(m // tm,),
        in_specs=[pl.BlockSpec((tm, wtot), lambda i: (i, 0)), wspec, wspec, tab, tab],
        out_specs=(pl.BlockSpec((tm, ATT_WIDTH), lambda i: (i, 0)),
                   pl.BlockSpec((tm, ATT_KV_WIDTH), lambda i: (i, 0)),
                   pl.BlockSpec((tm, 2 * ATT_KV_WIDTH), lambda i: (i, 0))),
        compiler_params=_params("arbitrary"),
        name="attention_prep_rope" if rope else "attention_prep",
    )(p_att, q_norm_w.reshape(1, -1), k_norm_w.reshape(1, -1), cos, sin)


def _attn_kernel(q_ref, k_ref, v_ref, o_ref, *, tk, nk):
    tq = q_ref.shape[1]
    qs = [q_ref[0, :, g * ATT_HEAD_DIM:(g + 1) * ATT_HEAD_DIM] for g in range(ATT_GROUP)]

    def body(c, carry):
        off = pl.multiple_of(c * tk, tk)
        kc = k_ref[0, pl.ds(off, tk), :]
        vc = v_ref[0, pl.ds(off, tk), :]
        out = []
        for g in range(ATT_GROUP):
            m, acc = carry[g]
            s = _dot_nt(qs[g], kc)
            mn = jnp.maximum(m, jnp.max(s, axis=-1, keepdims=True))
            p = jnp.exp2(s - mn)
            out.append((mn, jnp.exp2(m - mn) * acc + _dot(p.astype(BF16), vc)))
        return tuple(out)

    init = tuple((jnp.full((tq, 1), -jnp.inf, F32), jnp.zeros((tq, 2 * ATT_HEAD_DIM), F32))
                 for _ in range(ATT_GROUP))
    res = lax.fori_loop(0, nk, body, init)
    for g in range(ATT_GROUP):
        acc = res[g][1]
        o_ref[0, :, g * ATT_HEAD_DIM:(g + 1) * ATT_HEAD_DIM] = (
            acc[:, :ATT_HEAD_DIM] / acc[:, ATT_HEAD_DIM:]).astype(o_ref.dtype)


def gqa_attention(q, k, v):
    b, t, _ = q.shape
    s = k.shape[1]
    tq = _tile(t, (512, 256, 128))
    tk = s if s <= 1024 else _tile(s, (1408, 768, 512, 384, 256, 128))
    gwid = ATT_GROUP * ATT_HEAD_DIM
    return pl.pallas_call(
        functools.partial(_attn_kernel, tk=tk, nk=s // tk),
        out_shape=jax.ShapeDtypeStruct(q.shape, BF16),
        grid=(b, ATT_KV_HEADS, t // tq),
        in_specs=[pl.BlockSpec((1, tq, gwid), lambda i, h, j: (i, j, h)),
                  pl.BlockSpec((1, s, ATT_HEAD_DIM), lambda i, h, j: (i, 0, h)),
                  pl.BlockSpec((1, s, 2 * ATT_HEAD_DIM), lambda i, h, j: (i, 0, h))],
        out_specs=pl.BlockSpec((1, tq, gwid), lambda i, h, j: (i, j, h)),
        compiler_params=_params("arbitrary", "arbitrary", "arbitrary"),
        name="gqa_attention",
    )(q, k, v)


def _rope_tables(n_tokens):
    rows = n_tokens // GRID_W
    row = jnp.repeat(jnp.arange(rows, dtype=F32), GRID_W)
    col = jnp.tile(jnp.arange(GRID_W, dtype=F32), rows)
    axis_dim = ATT_HEAD_DIM // 2
    inv = ROPE_THETA ** (-jnp.arange(0, axis_dim, 2, dtype=F32) / axis_dim)
    ar, ac = row[:, None] * inv, col[:, None] * inv
    cos = jnp.concatenate([jnp.cos(ar), jnp.cos(ar), jnp.cos(ac), jnp.cos(ac)], axis=-1)
    sin = jnp.concatenate([-jnp.sin(ar), jnp.sin(ar), -jnp.sin(ac), jnp.sin(ac)], axis=-1)
    return cos, sin


def kernel(x, c, ctx, c_ctx, ada_w, ada_b, norm_mix_w, norm_ffn_w, w_in, hg_lb_logits, hg_norm_w, ssm_conv_w, ssm_conv_b, ssm_dt_bias, ssm_a_log, ssm_d, ssm_norm_w, attn_q_norm_w, attn_k_norm_w, w_branch, w_out, ffn_w_gate, ffn_w_up, ffn_w_down, moe_router, moe_w_gate, moe_w_up, moe_w_down, final_norm_w):
    bsz, seq, d = x.shape
    clen = ctx.shape[1]
    depth = ada_w.shape[0]
    m_lat, m_ctx = bsz * seq, bsz * clen

    lb_all = jnp.cumsum(jax.nn.softmax(hg_lb_logits.astype(F32), axis=0), axis=0)
    lb_all = lb_all - lb_all[0]
    cond = jnp.zeros((SUBLANES, d), F32).at[:bsz].set(c).at[bsz].set(c_ctx)
    cos, sin = _rope_tables(seq)

    e_hg = 5 * HG_WIDTH
    e_ssd = e_hg + SSM_WIDTH + SSM_CONV_CH
    e_dt = e_ssd + 2 * SSM_HEADS
    e_att = e_dt + ATT_WIDTH + 2 * ATT_KV_WIDTH

    zeros_hg = jnp.zeros((bsz, HG_HEADS, HG_DK, HG_DK), F32)
    zeros_ssd = jnp.zeros((bsz, SSM_HEADS // 2, 2 * SSM_HEAD_DIM, SSM_STATE), F32)

    x2 = x.reshape(m_lat, d)
    xc2 = ctx.reshape(m_ctx, d)
    for l in range(depth):
        need_ctx = l < depth - 1
        mod = ada_modulation(cond, ada_w[l], ada_b[l])
        mod_lat = mod[:bsz].reshape(bsz, 6, 1, d)
        mod_ctx = mod[bsz:bsz + 1].reshape(1, 6, 1, d)
        ml = [mod_lat[:, n] for n in range(6)]
        mc = [mod_ctx[:, n] for n in range(6)]

        wl = w_in[l]
        w_hg = wl[:, :e_hg].astype(BF16)
        w_ssd = wl[:, e_hg:e_ssd].astype(BF16)
        w_dt = jnp.zeros((d, LANES), BF16).at[:, :2 * SSM_HEADS].set(wl[:, e_ssd:e_dt].astype(BF16))
        w_att = wl[:, e_dt:e_att].astype(BF16)
        w_gate = wl[:, e_att:].astype(BF16)
        w_br = w_branch[l].astype(BF16)
        w_o = w_out[l].astype(BF16)

        prm = jnp.zeros((SUBLANES, LANES), F32)
        prm = prm.at[0, :2 * SSM_HEADS].set(ssm_dt_bias[l].reshape(-1).astype(F32))
        prm = prm.at[1, :2 * SSM_HEADS].set(-jnp.exp(ssm_a_log[l].reshape(-1).astype(F32)))
        prm = prm.at[2, :2 * SSM_HEADS].set(ssm_d[l].reshape(-1).astype(F32))

        def mixer_inputs(xs, nb, t, shift, scale):
            h = norm_modulate(xs.reshape(nb, t, d), norm_mix_w[l], shift, scale).reshape(nb * t, d)
            p_hg = matmul(h, w_hg, F32).reshape(nb, t, e_hg)
            p_ssd = matmul(h, w_ssd, F32).reshape(nb, t, e_ssd - e_hg)
            p_dt = matmul(h, w_dt, F32).reshape(nb, t, LANES)
            p_att = matmul(h, w_att, F32)
            xbc = ssd_conv_silu(p_ssd, ssm_conv_w[l], ssm_conv_b[l])
            return h, p_hg, p_ssd, p_dt, p_att, xbc

        h_c, phg_c, pssd_c, pdt_c, patt_c, xbc_c = mixer_inputs(xc2, bsz, clen, mc[0], mc[1])
        h_l, phg_l, pssd_l, pdt_l, patt_l, xbc_l = mixer_inputs(x2, bsz, seq, ml[0], ml[1])

        ofc, s_f = hgrn2_scan(phg_c, lb_all[l, 0], zeros_hg, False)
        obc, s_b = hgrn2_scan(phg_c, lb_all[l, 1], zeros_hg, True)
        ofl, _ = hgrn2_scan(phg_l, lb_all[l, 0], s_f, False)
        obl, _ = hgrn2_scan(phg_l, lb_all[l, 1], s_b, True)
        ya_l = hgrn2_readout(ofl.reshape(m_lat, -1), obl.reshape(m_lat, -1), phg_l.reshape(m_lat, -1), hg_norm_w[l])

        yfc, h_f = ssd_scan(xbc_c, pdt_c, prm, zeros_ssd, False)
        ybc, h_b = ssd_scan(xbc_c, pdt_c, prm, zeros_ssd, True)
        yfl, _ = ssd_scan(xbc_l, pdt_l, prm, h_f, False)
        ybl, _ = ssd_scan(xbc_l, pdt_l, prm, h_b, True)
        yb_l = ssd_readout(yfl.reshape(m_lat, -1), ybl.reshape(m_lat, -1), pssd_l.reshape(m_lat, -1), ssm_norm_w[l])

        q_c, k_c, v_c = attention_prep(patt_c, attn_q_norm_w[l], attn_k_norm_w[l], cos, sin, clen, False)
        q_l, k_l, v_l = attention_prep(patt_l, attn_q_norm_w[l], attn_k_norm_w[l], cos, sin, seq, True)
        k_c3, v_c3 = k_c.reshape(bsz, clen, -1), v_c.reshape(bsz, clen, -1)
        k_all = jnp.concatenate([k_c3, k_l.reshape(bsz, seq, -1)], axis=1)
        v_all = jnp.concatenate([v_c3, v_l.reshape(bsz, seq, -1)], axis=1)
        yc_l = gqa_attention(q_l.reshape(bsz, seq, -1), k_all, v_all).reshape(m_lat, -1)

        gates_l = matmul(h_l, w_gate, BF16, act="sigmoid")
        merged_l = merge_branches(ya_l, yb_l, yc_l, gates_l, w_br)
        x2 = matmul_residual(merged_l, w_o, x2, ml[2])

        if need_ctx:
            ya_c = hgrn2_readout(ofc.reshape(m_ctx, -1), obc.reshape(m_ctx, -1), phg_c.reshape(m_ctx, -1), hg_norm_w[l])
            yb_c = ssd_readout(yfc.reshape(m_ctx, -1), ybc.reshape(m_ctx, -1), pssd_c.reshape(m_ctx, -1), ssm_norm_w[l])
            yc_c = gqa_attention(q_c.reshape(bsz, clen, -1), k_c3, v_c3).reshape(m_ctx, -1)
            gates_c = matmul(h_c, w_gate, BF16, act="sigmoid")
            merged_c = merge_branches(ya_c, yb_c, yc_c, gates_c, w_br)
            xc2 = matmul_residual(merged_c, w_o, xc2, mc[2])

        if l % 2 == 0:
            wg = ffn_w_gate[l // 2].astype(BF16)
            wu = ffn_w_up[l // 2].astype(BF16)
            wd = ffn_w_down[l // 2].astype(BF16)
            hf = norm_modulate(x2.reshape(bsz, seq, d), norm_ffn_w[l], ml[3], ml[4]).reshape(m_lat, d)
            x2 = swiglu_residual(hf, wg, wu, wd, x2, ml[5])
            if need_ctx:
                hfc = norm_modulate(xc2.reshape(bsz, clen, d), norm_ffn_w[l], mc[3], mc[4]).reshape(m_ctx, d)
                xc2 = swiglu_residual(hfc, wg, wu, wd, xc2, mc[5])
        else:
            wg = moe_w_gate[l // 2].astype(BF16)
            wu = moe_w_up[l // 2].astype(BF16)
            wd = moe_w_down[l // 2].astype(BF16)
            x2 = moe_swiglu_residual(x2.reshape(bsz, seq, d), norm_ffn_w[l], ml[3], ml[4], moe_router[l // 2],
                                     wg, wu, wd, ml[5])
            if need_ctx:
                xc2 = moe_swiglu_residual(xc2.reshape(bsz, clen, d), norm_ffn_w[l],
                                          jnp.broadcast_to(mc[3], (bsz, 1, d)), jnp.broadcast_to(mc[4], (bsz, 1, d)),
                                          moe_router[l // 2], wg, wu, wd, mc[5])

    return rms_norm_rows(x2, final_norm_w).reshape(bsz, seq, d)
```
